```python
import math
import jax, jax.numpy as jnp
from jax import lax
import numpy as np

D_MODEL = 1024
BATCH = 8
SEQ = 4096
DEPTH = 1

HEAD_DIM = 64
N_Q_HEADS = 8
N_KV_HEADS = 2
Q_PER_KV = N_Q_HEADS // N_KV_HEADS
ATTN_WIDTH = N_Q_HEADS * HEAD_DIM
KV_WIDTH = N_KV_HEADS * HEAD_DIM
WINDOW = 128
BLOCK = 128
N_BUCKETS = 32
MAX_DISTANCE = 128
CONV_WIDTH = D_MODEL - ATTN_WIDTH
CONV_GROUPS = 8
CONV_K = 3
IN_PROJ_WIDTH = ATTN_WIDTH + 2 * KV_WIDTH + 3 * CONV_WIDTH
D_FF = -(-8 * D_MODEL // (3 * 256)) * 256
N_MOD = 6
EPS = 1e-6
NEG_INF = -1e30

kernel_name = "hymba_swa_sink_shortconv_adaln_block"


def rmsnorm(x, g):
    xf = x.astype(jnp.float32)
    y = xf * lax.rsqrt(jnp.mean(xf * xf, axis=-1, keepdims=True) + EPS)
    return (y * g.astype(jnp.float32)).astype(x.dtype)


def t5_bucket(dist):
    max_exact = N_BUCKETS // 2
    is_small = dist < max_exact
    d = jnp.maximum(dist, 1).astype(jnp.float32)
    large = max_exact + (jnp.log(d / max_exact) / math.log(MAX_DISTANCE / max_exact)
                         * (N_BUCKETS - max_exact)).astype(jnp.int32)
    large = jnp.minimum(large, N_BUCKETS - 1)
    return jnp.where(is_small, dist, large)


def banded_sink_attention(q, k, v, sinks, rel_bias):
    b, s = q.shape[0], q.shape[1]
    nb = s // BLOCK
    qb = q.reshape(b, nb, BLOCK, N_KV_HEADS, Q_PER_KV, HEAD_DIM)
    kb = k.reshape(b, nb, BLOCK, N_KV_HEADS, HEAD_DIM)
    vb = v.reshape(b, nb, BLOCK, N_KV_HEADS, HEAD_DIM)

    def with_prev(t):
        prev = jnp.concatenate([jnp.zeros_like(t[:, :1]), t[:, :-1]], axis=1)
        return jnp.concatenate([prev, t], axis=2)

    kw, vw = with_prev(kb), with_prev(vb)
    scores = jnp.einsum("bnqkgd,bnskd->bnkgqs", qb, kw).astype(jnp.float32) * (HEAD_DIM ** -0.5)

    qi = jnp.arange(BLOCK, dtype=jnp.int32)[:, None]
    sj = jnp.arange(2 * BLOCK, dtype=jnp.int32)[None, :]
    dist = qi + BLOCK - sj
    bias = rel_bias[t5_bucket(jnp.maximum(dist, 0))]
    bias = jnp.transpose(bias, (2, 0, 1)).reshape(N_KV_HEADS, Q_PER_KV, BLOCK, 2 * BLOCK)
    scores = scores + bias.astype(jnp.float32)

    in_window = (dist >= 0) & (dist < WINDOW)
    key_pos = jnp.arange(nb, dtype=jnp.int32)[:, None] * BLOCK - BLOCK + sj
    valid = in_window[None] & (key_pos >= 0)[:, None, :]
    scores = jnp.where(valid[None, :, None, None], scores, NEG_INF)

    sink = sinks.astype(jnp.float32).reshape(N_KV_HEADS, Q_PER_KV)[None, None, :, :, None, None]
    m = jnp.maximum(jnp.max(scores, axis=-1, keepdims=True), sink)
    p = jnp.exp(scores - m)
    p = p / (jnp.sum(p, axis=-1, keepdims=True) + jnp.exp(sink - m))
    out = jnp.einsum("bnkgqs,bnskd->bnqkgd", p.astype(v.dtype), vw)
    return out.reshape(b, s, ATTN_WIDTH)


def causal_short_conv(u, w):
    s = u.shape[1]
    up = jnp.pad(u, ((0, 0), (CONV_K - 1, 0), (0, 0)))
    return w[0] * up[:, 0:s] + w[1] * up[:, 1:s + 1] + w[2] * up[:, 2:s + 2]


def setup_inputs(seed: int = 0) -> dict:
    key = jax.random.key(seed)
    ks = jax.random.split(key, 20)
    f32 = jnp.float32
    n = lambda k, shape, scale: (jax.random.normal(k, shape, f32) * scale)
    return {
        "x": n(ks[0], (BATCH, SEQ, D_MODEL), 1.0),
        "c": n(ks[1], (BATCH, D_MODEL), 1.0),
        "rel_bias": n(ks[2], (N_BUCKETS, N_Q_HEADS), 0.5),
        "w_ada": n(ks[3], (DEPTH, D_MODEL, N_MOD * D_MODEL), D_MODEL ** -0.5),
        "b_ada": n(ks[4], (DEPTH, N_MOD * D_MODEL), 0.01),
        "g_norm1": 1.0 + n(ks[5], (DEPTH, D_MODEL), 0.02),
        "w_in": n(ks[6], (DEPTH, D_MODEL, IN_PROJ_WIDTH), D_MODEL ** -0.5),
        "sinks": n(ks[7], (DEPTH, N_Q_HEADS), 1.0),
        "conv_w": n(ks[8], (DEPTH, CONV_K, CONV_WIDTH), CONV_K ** -0.5),
        "g_attn_out": 1.0 + n(ks[9], (DEPTH, ATTN_WIDTH), 0.02),
        "g_conv_out": 1.0 + n(ks[10], (DEPTH, CONV_WIDTH), 0.02),
        "w_out": n(ks[11], (DEPTH, D_MODEL, D_MODEL), D_MODEL ** -0.5),
        "g_norm2": 1.0 + n(ks[12], (DEPTH, D_MODEL), 0.02),
        "w_gu": n(ks[13], (DEPTH, D_MODEL, 2 * D_FF), D_MODEL ** -0.5),
        "w_down": n(ks[14], (DEPTH, D_FF, D_MODEL), D_FF ** -0.5),
        "g_final": 1.0 + n(ks[15], (D_MODEL,), 0.02),
    }


def reference(x, c, rel_bias, w_ada, b_ada, g_norm1, w_in, sinks, conv_w, g_attn_out,
              g_conv_out, w_out, g_norm2, w_gu, w_down, g_final):
    b, s = x.shape[0], x.shape[1]
    cond = jax.nn.silu(c)
    splits = [ATTN_WIDTH, ATTN_WIDTH + KV_WIDTH, ATTN_WIDTH + 2 * KV_WIDTH,
              ATTN_WIDTH + 2 * KV_WIDTH + CONV_WIDTH, ATTN_WIDTH + 2 * KV_WIDTH + 2 * CONV_WIDTH]
    for l in range(DEPTH):
        mod = (cond @ w_ada[l] + b_ada[l])[:, None, :]
        sh1, sc1, g1, sh2, sc2, g2 = jnp.split(mod, N_MOD, axis=-1)

        h = rmsnorm(x, g_norm1[l]) * (1.0 + sc1) + sh1
        proj = h @ w_in[l]
        q, k, v, gate_b, gate_c, xc = jnp.split(proj, splits, axis=-1)
        q = q.reshape(b, s, N_Q_HEADS, HEAD_DIM)
        k = k.reshape(b, s, N_KV_HEADS, HEAD_DIM)
        v = v.reshape(b, s, N_KV_HEADS, HEAD_DIM)
        attn = banded_sink_attention(q, k, v, sinks[l], rel_bias)
        conv = gate_b * causal_short_conv(gate_c * xc, conv_w[l])
        merged = jnp.concatenate([rmsnorm(attn, g_attn_out[l]), rmsnorm(conv, g_conv_out[l])], axis=-1)
        x = x + g1 * (merged @ w_out[l])

        h2 = rmsnorm(x, g_norm2[l]) * (1.0 + sc2) + sh2
        gate, up = jnp.split(h2 @ w_gu[l], 2, axis=-1)
        x = x + g2 * ((jax.nn.silu(gate) * up) @ w_down[l])
    return rmsnorm(x, g_final)
```

```python
import functools
import math

import jax
import jax.numpy as jnp
import numpy as np
from jax import lax
from jax.experimental import pallas as pl
from jax.experimental.pallas import tpu as pltpu

D_MODEL = 1024
HEAD_DIM = 64
N_Q_HEADS = 8
N_KV_HEADS = 2
ATTN_WIDTH = N_Q_HEADS * HEAD_DIM
KV_WIDTH = N_KV_HEADS * HEAD_DIM
WINDOW = 128
BLOCK = 128
N_BUCKETS = 32
MAX_DISTANCE = 128
CONV_WIDTH = D_MODEL - ATTN_WIDTH
CONV_K = 3
IN_PROJ_WIDTH = ATTN_WIDTH + 2 * KV_WIDTH + 3 * CONV_WIDTH
D_FF = -(-8 * D_MODEL // (3 * 256)) * 256
N_MOD = 6
EPS = 1e-6
NEG_INF = -1e30

LANES = 128
SUBLANES = 8
MXU_DIM = 256
VMEM_BYTES = 64 * 1024 * 1024

F32 = jnp.float32
BF16 = jnp.bfloat16

TM_PROJ = 512
TQ_ATTN = 512
TM_FFN = 512
ADA_TN = 1024
FF_CHUNK = 2 * MXU_DIM


def _bucket_table():
    qi = np.arange(BLOCK, dtype=np.int32)[:, None]
    sj = np.arange(2 * BLOCK, dtype=np.int32)[None, :]
    dist = qi + BLOCK - sj
    d0 = np.maximum(dist, 0)
    max_exact = N_BUCKETS // 2
    d = np.maximum(d0, 1).astype(np.float32)
    large = max_exact + (np.log(d / np.float32(max_exact)) / np.float32(math.log(MAX_DISTANCE / max_exact))
                         * np.float32(N_BUCKETS - max_exact)).astype(np.int32)
    large = np.minimum(large, N_BUCKETS - 1)
    bucket = np.where(d0 < max_exact, d0, large)
    in_window = (dist >= 0) & (dist < WINDOW)
    return np.where(in_window, bucket, -1).astype(np.int32)


def _rms_scale(v, width):
    return lax.rsqrt(jnp.sum(v * v, axis=-1, keepdims=True) * (1.0 / width) + EPS)


def _adaln_kernel(c_ref, w_ref, b_ref, o_ref):
    c = c_ref[...]
    cond = c * jax.nn.sigmoid(c)
    o_ref[...] = jnp.dot(cond.astype(BF16), w_ref[...].astype(BF16),
                         preferred_element_type=F32) + b_ref[...]


def _adaln(c, w_ada, b_ada):
    b = c.shape[0]
    n = w_ada.shape[1]
    return pl.pallas_call(
        _adaln_kernel,
        grid=(n // ADA_TN,),
        in_specs=[pl.BlockSpec((b, D_MODEL), lambda j: (0, 0)),
                  pl.BlockSpec((D_MODEL, ADA_TN), lambda j: (0, j)),
                  pl.BlockSpec((1, ADA_TN), lambda j: (0, j))],
        out_specs=pl.BlockSpec((b, ADA_TN), lambda j: (0, j)),
        out_shape=jax.ShapeDtypeStruct((b, n), F32),
        name="adaln_mod",
    )(c, w_ada, b_ada.reshape(1, n))


def _inproj_kernel(x_ref, sh_ref, sc_ref, gn_ref, w_ref, cw_ref, gc_ref,
                   q_ref, kv_ref, cv_ref, ubuf):
    tm = x_ref.shape[1]
    x = x_ref[0]
    h = (x * _rms_scale(x, D_MODEL) * gn_ref[...]) * (1.0 + sc_ref[0]) + sh_ref[0]
    proj = jnp.dot(h.astype(BF16), w_ref[...], preferred_element_type=F32)

    q_ref[0] = (proj[:, :ATTN_WIDTH] * (HEAD_DIM ** -0.5)).astype(BF16)
    kv_ref[0] = proj[:, ATTN_WIDTH:ATTN_WIDTH + 2 * KV_WIDTH].astype(BF16)

    c0 = ATTN_WIDTH + 2 * KV_WIDTH
    gate_b = proj[:, c0:c0 + CONV_WIDTH]
    u = proj[:, c0 + CONV_WIDTH:c0 + 2 * CONV_WIDTH] * proj[:, c0 + 2 * CONV_WIDTH:]

    @pl.when(pl.program_id(1) == 0)
    def _():
        ubuf[0:SUBLANES, :] = jnp.zeros((SUBLANES, CONV_WIDTH), F32)

    ubuf[SUBLANES:SUBLANES + tm, :] = u
    u1 = ubuf[SUBLANES - 1:SUBLANES - 1 + tm, :]
    u2 = ubuf[SUBLANES - 2:SUBLANES - 2 + tm, :]
    conv = gate_b * (cw_ref[0:1, :] * u2 + cw_ref[1:2, :] * u1 + cw_ref[2:3, :] * u)
    cv_ref[0] = (conv * _rms_scale(conv, CONV_WIDTH) * gc_ref[...]).astype(BF16)
    ubuf[0:SUBLANES, :] = ubuf[tm:tm + SUBLANES, :]


def _inproj(x, mod3, g_norm1, w_in, conv_w, g_conv):
    b, s, _ = x.shape
    tm = TM_PROJ
    tok = lambda w: pl.BlockSpec((1, tm, w), lambda bi, i: (bi, i, 0))
    modspec = lambda k: pl.BlockSpec((1, 1, D_MODEL), lambda bi, i: (bi, 0, k))
    const = lambda shape: pl.BlockSpec(shape, lambda bi, i: (0,) * len(shape))
    return pl.pallas_call(
        _inproj_kernel,
        grid=(b, s // tm),
        in_specs=[tok(D_MODEL), modspec(0), modspec(1), const((1, D_MODEL)),
                  const((D_MODEL, IN_PROJ_WIDTH)), const((CONV_K, CONV_WIDTH)), const((1, CONV_WIDTH))],
        out_specs=[tok(ATTN_WIDTH), tok(2 * KV_WIDTH), tok(CONV_WIDTH)],
        out_shape=[jax.ShapeDtypeStruct((b, s, ATTN_WIDTH), BF16),
                   jax.ShapeDtypeStruct((b, s, 2 * KV_WIDTH), BF16),
                   jax.ShapeDtypeStruct((b, s, CONV_WIDTH), BF16)],
        scratch_shapes=[pltpu.VMEM((tm + SUBLANES, CONV_WIDTH), F32)],
        compiler_params=pltpu.CompilerParams(
            dimension_semantics=("arbitrary", "arbitrary"),
            vmem_limit_bytes=48 * 1024 * 1024),
        name="inproj_conv",
    )(x, mod3, mod3, g_norm1, w_in, conv_w, g_conv)


def _attn_kernel(sinks_ref, relb_ref, bucket_ref, q_ref, kv_ref, kvp_ref, cv_ref, x_ref, g1_ref,
                 ga_ref, wo_ref, o_ref, biasm, kvbuf, attn_sc, merged):
    tq = q_ref.shape[1]
    nblk = tq // BLOCK
    i = pl.program_id(1)

    @pl.when((pl.program_id(0) == 0) & (i == 0))
    def _():
        bk = bucket_ref[...]
        prev_half = lax.broadcasted_iota(jnp.int32, (BLOCK, 2 * BLOCK), 1) < BLOCK
        for h in range(N_Q_HEADS):
            acc = jnp.full((BLOCK, 2 * BLOCK), NEG_INF, F32)
            for bb in range(N_BUCKETS):
                acc = jnp.where(bk == bb, relb_ref[bb, h], acc)
            biasm[h] = acc
            biasm[N_Q_HEADS + h] = jnp.where(prev_half, NEG_INF, acc)

    kvbuf[0:BLOCK, :] = kvp_ref[0]
    kvbuf[BLOCK:BLOCK + tq, :] = kv_ref[0]
    lo = lax.broadcasted_iota(jnp.int32, (2 * BLOCK, LANES), 1) < HEAD_DIM
    zero = jnp.zeros((2 * BLOCK, LANES), BF16)

    def block(j, carry):
        r0 = pl.multiple_of(j * BLOCK, BLOCK)
        table = jnp.where((i == 0) & (j == 0), N_Q_HEADS, 0)
        qblk = q_ref[0, pl.ds(r0, BLOCK), :]
        kvw = kvbuf[pl.ds(r0, 2 * BLOCK), :]
        k = kvw[:, :LANES]
        v = kvw[:, LANES:]
        kr = pltpu.roll(k, HEAD_DIM, 1)
        vr = pltpu.roll(v, HEAD_DIM, 1)
        ssq = jnp.zeros((BLOCK, 1), F32)
        for kh in range(N_KV_HEADS):
            if kh == 0:
                k_lo, k_hi = jnp.where(lo, k, zero), jnp.where(lo, zero, kr)
                v_lo, v_hi = jnp.where(lo, v, zero), jnp.where(lo, zero, vr)
            else:
                k_lo, k_hi = jnp.where(lo, kr, zero), jnp.where(lo, zero, k)
                v_lo, v_hi = jnp.where(lo, vr, zero), jnp.where(lo, zero, v)
            for pp in range(2):
                pair = kh * 2 + pp
                qp = qblk[:, pair * LANES:(pair + 1) * LANES]
                o = jnp.zeros((BLOCK, LANES), F32)
                for t, (kx, vx) in enumerate(((k_lo, v_lo), (k_hi, v_hi))):
                    h = pair * 2 + t
                    sink = sinks_ref[h]
                    sc = lax.dot_general(qp, kx, (((1,), (1,)), ((), ())), preferred_element_type=F32)
                    sc = sc + biasm[table + h]
                    m = jnp.maximum(jnp.max(sc, axis=-1, keepdims=True), sink)
                    e = jnp.exp(sc - m)
                    den = jnp.sum(e, axis=-1, keepdims=True) + jnp.exp(sink - m)
                    pv = jnp.dot(e.astype(BF16), vx, preferred_element_type=F32)
                    o = o + pv * (1.0 / den)
                attn_sc[:, pair * LANES:(pair + 1) * LANES] = o
                ssq = ssq + jnp.sum(o * o, axis=-1, keepdims=True)
        scale = lax.rsqrt(ssq * (1.0 / ATTN_WIDTH) + EPS)
        merged[pl.ds(r0, BLOCK), 0:ATTN_WIDTH] = (attn_sc[...] * scale * ga_ref[...]).astype(BF16)
        return carry

    lax.fori_loop(0, nblk, block, 0)
    merged[:, ATTN_WIDTH:] = cv_ref[0]
    y = jnp.dot(merged[...], wo_ref[...], preferred_element_type=F32)
    o_ref[0] = x_ref[0] + g1_ref[0] * y


def _attn_outproj(q, kv, convn, x, mod3, sinks, rel_bias, g_attn, w_out):
    b, s, _ = x.shape
    tq = TQ_ATTN
    nblk = tq // BLOCK
    tok = lambda w: pl.BlockSpec((1, tq, w), lambda bi, i: (bi, i, 0))
    const = lambda shape: pl.BlockSpec(shape, lambda bi, i: (0,) * len(shape))
    smem = pl.BlockSpec(memory_space=pltpu.SMEM)
    bucket = jnp.asarray(_bucket_table())
    return pl.pallas_call(
        _attn_kernel,
        grid=(b, s // tq),
        in_specs=[smem, smem, const((BLOCK, 2 * BLOCK)),
                  tok(ATTN_WIDTH), tok(2 * KV_WIDTH),
                  pl.BlockSpec((1, BLOCK, 2 * KV_WIDTH),
                               lambda bi, i: (bi, jnp.maximum(i * nblk - 1, 0), 0)),
                  tok(CONV_WIDTH), tok(D_MODEL),
                  pl.BlockSpec((1, 1, D_MODEL), lambda bi, i: (bi, 0, 2)),
                  const((1, ATTN_WIDTH)), const((D_MODEL, D_MODEL))],
        out_specs=tok(D_MODEL),
        out_shape=jax.ShapeDtypeStruct((b, s, D_MODEL), F32),
        scratch_shapes=[pltpu.VMEM((2 * N_Q_HEADS, BLOCK, 2 * BLOCK), F32),
                        pltpu.VMEM((tq + BLOCK, 2 * KV_WIDTH), BF16),
                        pltpu.VMEM((BLOCK, ATTN_WIDTH), F32),
                        pltpu.VMEM((tq, D_MODEL), BF16)],
        compiler_params=pltpu.CompilerParams(
            dimension_semantics=("arbitrary", "arbitrary"),
            vmem_limit_bytes=48 * 1024 * 1024),
        name="attn_outproj",
    )(sinks, rel_bias, bucket, q, kv, kv, convn, x, mod3, g_attn, w_out)


def _ffn_kernel(x_ref, sh_ref, sc_ref, g2_ref, gn_ref, wgu_ref, wd_ref, gf_ref, o_ref, act, *, final_norm):
    x = x_ref[0]
    h = ((x * _rms_scale(x, D_MODEL) * gn_ref[...]) * (1.0 + sc_ref[0]) + sh_ref[0]).astype(BF16)
    for c0 in range(0, D_FF, FF_CHUNK):
        c1 = min(c0 + FF_CHUNK, D_FF)
        gate = jnp.dot(h, wgu_ref[:, c0:c1], preferred_element_type=F32)
        up = jnp.dot(h, wgu_ref[:, D_FF + c0:D_FF + c1], preferred_element_type=F32)
        act[:, c0:c1] = (gate * jax.nn.sigmoid(gate) * up).astype(BF16)
    y = jnp.dot(act[...], wd_ref[...], preferred_element_type=F32)
    x2 = x + g2_ref[0] * y
    o_ref[0] = x2 * _rms_scale(x2, D_MODEL) * gf_ref[...] if final_norm else x2


def _ffn(x1, mod3, g_norm2, w_gu, w_down, g_final, final_norm):
    b, s, _ = x1.shape
    tm = TM_FFN
    tok = pl.BlockSpec((1, tm, D_MODEL), lambda bi, i: (bi, i, 0))
    modspec = lambda k: pl.BlockSpec((1, 1, D_MODEL), lambda bi, i: (bi, 0, k))
    const = lambda shape: pl.BlockSpec(shape, lambda bi, i: (0,) * len(shape))
    resident = lambda shape: pl.BlockSpec(shape, lambda bi, i: (0,) * len(shape),
                                          pipeline_mode=pl.Buffered(1))
    return pl.pallas_call(
        functools.partial(_ffn_kernel, final_norm=final_norm),
        grid=(b, s // tm),
        in_specs=[tok, modspec(3), modspec(4), modspec(5), const((1, D_MODEL)),
                  resident((D_MODEL, 2 * D_FF)), resident((D_FF, D_MODEL)), const((1, D_MODEL))],
        out_specs=tok,
        out_shape=jax.ShapeDtypeStruct((b, s, D_MODEL), F32),
        scratch_shapes=[pltpu.VMEM((tm, D_FF), BF16)],
        compiler_params=pltpu.CompilerParams(
            dimension_semantics=("parallel", "parallel"),
            vmem_limit_bytes=56 * 1024 * 1024),
        name="ffn_final",
    )(x1, mod3, mod3, mod3, g_norm2, w_gu, w_down, g_final)


def kernel(x, c, rel_bias, w_ada, b_ada, g_norm1, w_in, sinks, conv_w, g_attn_out, g_conv_out,
           w_out, g_norm2, w_gu, w_down, g_final):
    depth = w_ada.shape[0]
    b = x.shape[0]
    for l in range(depth):
        mod3 = _adaln(c, w_ada[l], b_ada[l]).reshape(b, 1, N_MOD * D_MODEL)
        q, kv, convn = _inproj(x, mod3, g_norm1[l].reshape(1, D_MODEL), w_in[l].astype(BF16),
                               conv_w[l], g_conv_out[l].reshape(1, CONV_WIDTH))
        x1 = _attn_outproj(q, kv, convn, x, mod3, sinks[l], rel_bias,
                           g_attn_out[l].reshape(1, ATTN_WIDTH), w_out[l].astype(BF16))
        x = _ffn(x1, mod3, g_norm2[l].reshape(1, D_MODEL), w_gu[l].astype(BF16),
                 w_down[l].astype(BF16), g_final.reshape(1, D_MODEL), final_norm=(l == depth - 1))
    return x
```

```python
import functools
import math

import jax
import jax.numpy as jnp
import numpy as np
from jax import lax
from jax.experimental import pallas as pl
from jax.experimental.pallas import tpu as pltpu

D_MODEL = 1024
HEAD_DIM = 64
N_Q_HEADS = 8
N_KV_HEADS = 2
ATTN_WIDTH = N_Q_HEADS * HEAD_DIM
KV_WIDTH = N_KV_HEADS * HEAD_DIM
WINDOW = 128
BLOCK = 128
N_BUCKETS = 32
MAX_DISTANCE = 128
CONV_WIDTH = D_MODEL - ATTN_WIDTH
CONV_K = 3
IN_PROJ_WIDTH = ATTN_WIDTH + 2 * KV_WIDTH + 3 * CONV_WIDTH
D_FF = -(-8 * D_MODEL // (3 * 256)) * 256
N_MOD = 6
EPS = 1e-6
NEG_INF = -1e30
LOG2E = math.log2(math.e)

LANES = 128
SUBLANES = 8
MXU_DIM = 256
VMEM_BYTES = 64 * 1024 * 1024

F32 = jnp.float32
BF16 = jnp.bfloat16

TM_PROJ = 512
TQ_ATTN = 512
TM_FFN = 512
ADA_TN = 1024
FF_CHUNK = 2 * MXU_DIM


def _bucket_table():
    qi = np.arange(BLOCK, dtype=np.int32)[:, None]
    sj = np.arange(2 * BLOCK, dtype=np.int32)[None, :]
    dist = qi + BLOCK - sj
    d0 = np.maximum(dist, 0)
    max_exact = N_BUCKETS // 2
    d = np.maximum(d0, 1).astype(np.float32)
    large = max_exact + (np.log(d / np.float32(max_exact)) / np.float32(math.log(MAX_DISTANCE / max_exact))
                         * np.float32(N_BUCKETS - max_exact)).astype(np.int32)
    large = np.minimum(large, N_BUCKETS - 1)
    bucket = np.where(d0 < max_exact, d0, large)
    in_window = (dist >= 0) & (dist < WINDOW)
    return np.where(in_window, bucket, -1).astype(np.int32)


def _rms_scale(v, width):
    return lax.rsqrt(jnp.sum(v * v, axis=-1, keepdims=True) * (1.0 / width) + EPS)


def _adaln_kernel(c_ref, w_ref, b_ref, o_ref):
    c = c_ref[...]
    cond = c * jax.nn.sigmoid(c)
    o_ref[...] = jnp.dot(cond.astype(BF16), w_ref[...].astype(BF16),
                         preferred_element_type=F32) + b_ref[...]


def _adaln(c, w_ada, b_ada):
    b = c.shape[0]
    n = w_ada.shape[1]
    return pl.pallas_call(
        _adaln_kernel,
        grid=(n // ADA_TN,),
        in_specs=[pl.BlockSpec((b, D_MODEL), lambda j: (0, 0)),
                  pl.BlockSpec((D_MODEL, ADA_TN), lambda j: (0, j)),
                  pl.BlockSpec((1, ADA_TN), lambda j: (0, j))],
        out_specs=pl.BlockSpec((b, ADA_TN), lambda j: (0, j)),
        out_shape=jax.ShapeDtypeStruct((b, n), F32),
        name="adaln_mod",
    )(c, w_ada, b_ada.reshape(1, n))


def _inproj_kernel(x_ref, sh_ref, sc_ref, gn_ref, w_ref, cw_ref, gc_ref,
                   q_ref, kv_ref, cv_ref, ubuf):
    tm = x_ref.shape[1]
    x = x_ref[0]
    h = (x * _rms_scale(x, D_MODEL) * gn_ref[...]) * (1.0 + sc_ref[0]) + sh_ref[0]
    proj = jnp.dot(h.astype(BF16), w_ref[...], preferred_element_type=F32)

    q_ref[0] = (proj[:, :ATTN_WIDTH] * (HEAD_DIM ** -0.5 * LOG2E)).astype(BF16)
    kv_ref[0] = proj[:, ATTN_WIDTH:ATTN_WIDTH + 2 * KV_WIDTH].astype(BF16)

    c0 = ATTN_WIDTH + 2 * KV_WIDTH
    gate_b = proj[:, c0:c0 + CONV_WIDTH]
    u = proj[:, c0 + CONV_WIDTH:c0 + 2 * CONV_WIDTH] * proj[:, c0 + 2 * CONV_WIDTH:]

    @pl.when(pl.program_id(1) == 0)
    def _():
        ubuf[0:SUBLANES, :] = jnp.zeros((SUBLANES, CONV_WIDTH), F32)

    ubuf[SUBLANES:SUBLANES + tm, :] = u
    u1 = ubuf[SUBLANES - 1:SUBLANES - 1 + tm, :]
    u2 = ubuf[SUBLANES - 2:SUBLANES - 2 + tm, :]
    conv = gate_b * (cw_ref[0:1, :] * u2 + cw_ref[1:2, :] * u1 + cw_ref[2:3, :] * u)
    cv_ref[0] = (conv * _rms_scale(conv, CONV_WIDTH) * gc_ref[...]).astype(BF16)
    ubuf[0:SUBLANES, :] = ubuf[tm:tm + SUBLANES, :]


def _inproj(x, mod3, g_norm1, w_in, conv_w, g_conv):
    b, s, _ = x.shape
    tm = TM_PROJ
    tok = lambda w: pl.BlockSpec((1, tm, w), lambda bi, i: (bi, i, 0))
    modspec = lambda k: pl.BlockSpec((1, 1, D_MODEL), lambda bi, i: (bi, 0, k))
    const = lambda shape: pl.BlockSpec(shape, lambda bi, i: (0,) * len(shape))
    return pl.pallas_call(
        _inproj_kernel,
        grid=(b, s // tm),
        in_specs=[tok(D_MODEL), modspec(0), modspec(1), const((1, D_MODEL)),
                  const((D_MODEL, IN_PROJ_WIDTH)), const((CONV_K, CONV_WIDTH)), const((1, CONV_WIDTH))],
        out_specs=[tok(ATTN_WIDTH), tok(2 * KV_WIDTH), tok(CONV_WIDTH)],
        out_shape=[jax.ShapeDtypeStruct((b, s, ATTN_WIDTH), BF16),
                   jax.ShapeDtypeStruct((b, s, 2 * KV_WIDTH), BF16),
                   jax.ShapeDtypeStruct((b, s, CONV_WIDTH), BF16)],
        scratch_shapes=[pltpu.VMEM((tm + SUBLANES, CONV_WIDTH), F32)],
        compiler_params=pltpu.CompilerParams(
            dimension_semantics=("arbitrary", "arbitrary"),
            vmem_limit_bytes=48 * 1024 * 1024),
        name="inproj_conv",
    )(x, mod3, mod3, g_norm1, w_in, conv_w, g_conv)


def _attn_kernel(sinks_ref, relb_ref, bucket_ref, q_ref, kv_ref, kvp_ref, cv_ref, x_ref, g1_ref,
                 ga_ref, wo_ref, o_ref, biasm, kvbuf, attn_sc, merged):
    tq = q_ref.shape[1]
    nblk = tq // BLOCK
    i = pl.program_id(1)
    n_pairs = N_Q_HEADS // 2

    @pl.when((pl.program_id(0) == 0) & (i == 0))
    def _():
        bk = bucket_ref[...]
        col = lax.broadcasted_iota(jnp.int32, (BLOCK, 2 * BLOCK), 1)
        for h in range(N_Q_HEADS):
            acc = jnp.full((BLOCK, 2 * BLOCK), NEG_INF, F32)
            for bb in range(N_BUCKETS):
                acc = jnp.where(bk == bb, relb_ref[bb, h] * LOG2E, acc)
            first = jnp.where(col < BLOCK, NEG_INF, acc)
            sink = sinks_ref[h] * LOG2E
            c0 = (h % 2) * 2 * BLOCK
            biasm[h // 2, :, c0:c0 + 2 * BLOCK] = jnp.where(col == 0, sink, acc)
            biasm[n_pairs + h // 2, :, c0:c0 + 2 * BLOCK] = jnp.where(col == 0, sink, first)

    kvbuf[0:BLOCK, :] = kvp_ref[0]
    kvbuf[BLOCK:BLOCK + tq, :] = kv_ref[0]
    lane = lax.broadcasted_iota(jnp.int32, (2 * BLOCK, LANES), 1)
    real_key = lax.broadcasted_iota(jnp.int32, (2 * BLOCK, LANES), 0) > 0
    keep_lo = (lane < HEAD_DIM) & real_key
    keep_hi = (lane >= HEAD_DIM) & real_key
    zero = jnp.zeros((2 * BLOCK, LANES), BF16)
    ones_lo = jnp.where(lane < HEAD_DIM, 1.0, 0.0).astype(BF16)
    ones_hi = jnp.where(lane >= HEAD_DIM, 1.0, 0.0).astype(BF16)

    def block(j, carry):
        r0 = pl.multiple_of(j * BLOCK, BLOCK)
        table = jnp.where((i == 0) & (j == 0), n_pairs, 0)
        qblk = q_ref[0, pl.ds(r0, BLOCK), :]
        kvw = kvbuf[pl.ds(r0, 2 * BLOCK), :]
        k = kvw[:, :LANES]
        v = kvw[:, LANES:]
        kr = pltpu.roll(k, HEAD_DIM, 1)
        vr = pltpu.roll(v, HEAD_DIM, 1)
        osq = jnp.zeros((BLOCK, LANES), F32)
        for kh in range(N_KV_HEADS):
            k_lo, k_hi = (k, kr) if kh == 0 else (kr, k)
            v_lo, v_hi = (v, vr) if kh == 0 else (vr, v)
            kx = jnp.concatenate([jnp.where(keep_lo, k_lo, zero), jnp.where(keep_hi, k_hi, zero)], axis=0)
            vx = jnp.concatenate(
                [jnp.concatenate([jnp.where(keep_lo, v_lo, zero), ones_lo], axis=1),
                 jnp.concatenate([jnp.where(keep_hi, v_hi, zero), ones_hi], axis=1)], axis=0)
            for pp in range(2):
                pair = kh * 2 + pp
                qp = qblk[:, pair * LANES:(pair + 1) * LANES]
                sc = lax.dot_general(qp, kx, (((1,), (1,)), ((), ())), preferred_element_type=F32)
                sc = sc + biasm[table + pair]
                es = []
                for t in range(2):
                    st = sc[:, t * 2 * BLOCK:(t + 1) * 2 * BLOCK]
                    m = jnp.max(st, axis=-1, keepdims=True)
                    es.append(jnp.exp2(st - m).astype(BF16))
                pv = jnp.dot(jnp.concatenate(es, axis=1), vx, preferred_element_type=F32)
                o = pv[:, :LANES] * (1.0 / pv[:, LANES:])
                attn_sc[:, pair * LANES:(pair + 1) * LANES] = o
                osq = osq + o * o
        scale = lax.rsqrt(jnp.sum(osq, axis=-1, keepdims=True) * (1.0 / ATTN_WIDTH) + EPS)
        merged[pl.ds(r0, BLOCK), 0:ATTN_WIDTH] = (attn_sc[...] * scale * ga_ref[...]).astype(BF16)
        return carry

    lax.fori_loop(0, nblk, block, 0, unroll=True)
    merged[:, ATTN_WIDTH:] = cv_ref[0]
    y = jnp.dot(merged[...], wo_ref[...], preferred_element_type=F32)
    o_ref[0] = x_ref[0] + g1_ref[0] * y


def _attn_outproj(q, kv, convn, x, mod3, sinks, rel_bias, g_attn, w_out):
    b, s, _ = x.shape
    tq = TQ_ATTN
    nblk = tq // BLOCK
    tok = lambda w: pl.BlockSpec((1, tq, w), lambda bi, i: (bi, i, 0))
    const = lambda shape: pl.BlockSpec(shape, lambda bi, i: (0,) * len(shape))
    smem = pl.BlockSpec(memory_space=pltpu.SMEM)
    bucket = jnp.asarray(_bucket_table())
    return pl.pallas_call(
        _attn_kernel,
        grid=(b, s // tq),
        in_specs=[smem, smem, const((BLOCK, 2 * BLOCK)),
                  tok(ATTN_WIDTH), tok(2 * KV_WIDTH),
                  pl.BlockSpec((1, BLOCK, 2 * KV_WIDTH),
                               lambda bi, i: (bi, jnp.maximum(i * nblk - 1, 0), 0)),
                  tok(CONV_WIDTH), tok(D_MODEL),
                  pl.BlockSpec((1, 1, D_MODEL), lambda bi, i: (bi, 0, 2)),
                  const((1, ATTN_WIDTH)), const((D_MODEL, D_MODEL))],
        out_specs=tok(D_MODEL),
        out_shape=jax.ShapeDtypeStruct((b, s, D_MODEL), F32),
        scratch_shapes=[pltpu.VMEM((N_Q_HEADS, BLOCK, 4 * BLOCK), F32),
                        pltpu.VMEM((tq + BLOCK, 2 * KV_WIDTH), BF16),
                        pltpu.VMEM((BLOCK, ATTN_WIDTH), F32),
                        pltpu.VMEM((tq, D_MODEL), BF16)],
        compiler_params=pltpu.CompilerParams(
            dimension_semantics=("arbitrary", "arbitrary"),
            vmem_limit_bytes=48 * 1024 * 1024),
        name="attn_outproj",
    )(sinks, rel_bias, bucket, q, kv, kv, convn, x, mod3, g_attn, w_out)


def _ffn_kernel(x_ref, sh_ref, sc_ref, g2_ref, gn_ref, wgu_ref, wd_ref, gf_ref, o_ref, act, *, final_norm):
    x = x_ref[0]
    h = ((x * _rms_scale(x, D_MODEL) * gn_ref[...]) * (1.0 + sc_ref[0]) + sh_ref[0]).astype(BF16)
    for c0 in range(0, D_FF, FF_CHUNK):
        c1 = min(c0 + FF_CHUNK, D_FF)
        gate = jnp.dot(h, wgu_ref[:, c0:c1], preferred_element_type=F32)
        up = jnp.dot(h, wgu_ref[:, D_FF + c0:D_FF + c1], preferred_element_type=F32)
        act[:, c0:c1] = (gate * jax.nn.sigmoid(gate) * up).astype(BF16)
    y = jnp.dot(act[...], wd_ref[...], preferred_element_type=F32)
    x2 = x + g2_ref[0] * y
    o_ref[0] = x2 * _rms_scale(x2, D_MODEL) * gf_ref[...] if final_norm else x2


def _ffn(x1, mod3, g_norm2, w_gu, w_down, g_final, final_norm):
    b, s, _ = x1.shape
    tm = TM_FFN
    tok = pl.BlockSpec((1, tm, D_MODEL), lambda bi, i: (bi, i, 0))
    modspec = lambda k: pl.BlockSpec((1, 1, D_MODEL), lambda bi, i: (bi, 0, k))
    const = lambda shape: pl.BlockSpec(shape, lambda bi, i: (0,) * len(shape))
    resident = lambda shape: pl.BlockSpec(shape, lambda bi, i: (0,) * len(shape),
                                          pipeline_mode=pl.Buffered(1))
    return pl.pallas_call(
        functools.partial(_ffn_kernel, final_norm=final_norm),
        grid=(b, s // tm),
        in_specs=[tok, modspec(3), modspec(4), modspec(5), const((1, D_MODEL)),
                  resident((D_MODEL, 2 * D_FF)), resident((D_FF, D_MODEL)), const((1, D_MODEL))],
        out_specs=tok,
        out_shape=jax.ShapeDtypeStruct((b, s, D_MODEL), F32),
        scratch_shapes=[pltpu.VMEM((tm, D_FF), BF16)],
        compiler_params=pltpu.CompilerParams(
            dimension_semantics=("parallel", "parallel"),
            vmem_limit_bytes=56 * 1024 * 1024),
        name="ffn_final",
    )(x1, mod3, mod3, mod3, g_norm2, w_gu, w_down, g_final)


def kernel(x, c, rel_bias, w_ada, b_ada, g_norm1, w_in, sinks, conv_w, g_attn_out, g_conv_out,
           w_out, g_norm2, w_gu, w_down, g_final):
    depth = w_ada.shape[0]
    b = x.shape[0]
    for l in range(depth):
        mod3 = _adaln(c, w_ada[l], b_ada[l]).reshape(b, 1, N_MOD * D_MODEL)
        q, kv, convn = _inproj(x, mod3, g_norm1[l].reshape(1, D_MODEL), w_in[l].astype(BF16),
                               conv_w[l], g_conv_out[l].reshape(1, CONV_WIDTH))
        x1 = _attn_outproj(q, kv, convn, x, mod3, sinks[l], rel_bias,
                           g_attn_out[l].reshape(1, ATTN_WIDTH), w_out[l].astype(BF16))
        x = _ffn(x1, mod3, g_norm2[l].reshape(1, D_MODEL), w_gu[l].astype(BF16),
                 w_down[l].astype(BF16), g_final.reshape(1, D_MODEL), final_norm=(l == depth - 1))
    return x
```

```python
import functools
import math

import jax
import jax.numpy as jnp
import numpy as np
from jax import lax
from jax.experimental import pallas as pl
from jax.experimental.pallas import tpu as pltpu

D_MODEL = 1024
HEAD_DIM = 64
N_Q_HEADS = 8
N_KV_HEADS = 2
ATTN_WIDTH = N_Q_HEADS * HEAD_DIM
KV_WIDTH = N_KV_HEADS * HEAD_DIM
WINDOW = 128
BLOCK = 128
N_BUCKETS = 32
MAX_DISTANCE = 128
CONV_WIDTH = D_MODEL - ATTN_WIDTH
CONV_K = 3
IN_PROJ_WIDTH = ATTN_WIDTH + 2 * KV_WIDTH + 3 * CONV_WIDTH
D_FF = -(-8 * D_MODEL // (3 * 256)) * 256
N_MOD = 6
EPS = 1e-6
NEG_INF = -1e30
LOG2E = math.log2(math.e)

LANES = 128
SUBLANES = 8
MXU_DIM = 256
VMEM_BYTES = 64 * 1024 * 1024

F32 = jnp.float32
BF16 = jnp.bfloat16

TM_PROJ = 1024
PROJ_SUB = 256
TQ_ATTN = 512
OUTPROJ_ROWS = 256
TM_FFN = 512
FFN_SUB = 256
ADA_TN = 1024
FF_CHUNK = 2 * MXU_DIM


def _bucket_table():
    qi = np.arange(BLOCK, dtype=np.int32)[:, None]
    sj = np.arange(2 * BLOCK, dtype=np.int32)[None, :]
    dist = qi + BLOCK - sj
    d0 = np.maximum(dist, 0)
    max_exact = N_BUCKETS // 2
    d = np.maximum(d0, 1).astype(np.float32)
    large = max_exact + (np.log(d / np.float32(max_exact)) / np.float32(math.log(MAX_DISTANCE / max_exact))
                         * np.float32(N_BUCKETS - max_exact)).astype(np.int32)
    large = np.minimum(large, N_BUCKETS - 1)
    bucket = np.where(d0 < max_exact, d0, large)
    in_window = (dist >= 0) & (dist < WINDOW)
    return np.where(in_window, bucket, -1).astype(np.int32)


def _rms_scale(v, width):
    return lax.rsqrt(jnp.sum(v * v, axis=-1, keepdims=True) * (1.0 / width) + EPS)


def _adaln_kernel(c_ref, w_ref, b_ref, o_ref):
    c = c_ref[...]
    cond = c * jax.nn.sigmoid(c)
    o_ref[...] = jnp.dot(cond.astype(BF16), w_ref[...].astype(BF16),
                         preferred_element_type=F32) + b_ref[...]


def _adaln(c, w_ada, b_ada):
    b = c.shape[0]
    n = w_ada.shape[1]
    return pl.pallas_call(
        _adaln_kernel,
        grid=(n // ADA_TN,),
        in_specs=[pl.BlockSpec((b, D_MODEL), lambda j: (0, 0)),
                  pl.BlockSpec((D_MODEL, ADA_TN), lambda j: (0, j)),
                  pl.BlockSpec((1, ADA_TN), lambda j: (0, j))],
        out_specs=pl.BlockSpec((b, ADA_TN), lambda j: (0, j)),
        out_shape=jax.ShapeDtypeStruct((b, n), F32),
        name="adaln_mod",
    )(c, w_ada, b_ada.reshape(1, n))


def _inproj_kernel(x_ref, sh_ref, sc_ref, gn_ref, w_ref, cw_ref, gc_ref,
                   q_ref, kv_ref, cv_ref, ubuf):
    tm = x_ref.shape[1]

    @pl.when(pl.program_id(1) == 0)
    def _():
        ubuf[0:SUBLANES, :] = jnp.zeros((SUBLANES, CONV_WIDTH), F32)

    gain = gn_ref[...] * (1.0 + sc_ref[0])
    shift = sh_ref[0]
    subs = [slice(r0, r0 + PROJ_SUB) for r0 in range(0, tm, PROJ_SUB)]
    hs = []
    for rows in subs:
        x = x_ref[0, rows, :]
        hs.append(((x * _rms_scale(x, D_MODEL)) * gain + shift).astype(BF16))
    c0 = ATTN_WIDTH + 2 * KV_WIDTH
    for rows, h in zip(subs, hs):
        proj = jnp.dot(h, w_ref[...], preferred_element_type=F32)
        q_ref[0, rows, :] = (proj[:, :ATTN_WIDTH] * (HEAD_DIM ** -0.5 * LOG2E)).astype(BF16)
        kv_ref[0, rows, :] = proj[:, ATTN_WIDTH:c0].astype(BF16)
        gate_b = proj[:, c0:c0 + CONV_WIDTH]
        u = proj[:, c0 + CONV_WIDTH:c0 + 2 * CONV_WIDTH] * proj[:, c0 + 2 * CONV_WIDTH:]
        r0 = rows.start + SUBLANES
        ubuf[r0:r0 + PROJ_SUB, :] = u
        u1 = ubuf[r0 - 1:r0 - 1 + PROJ_SUB, :]
        u2 = ubuf[r0 - 2:r0 - 2 + PROJ_SUB, :]
        conv = gate_b * (cw_ref[0:1, :] * u2 + cw_ref[1:2, :] * u1 + cw_ref[2:3, :] * u)
        cv_ref[0, rows, :] = (conv * _rms_scale(conv, CONV_WIDTH) * gc_ref[...]).astype(BF16)
    ubuf[0:SUBLANES, :] = ubuf[tm:tm + SUBLANES, :]


def _inproj(x, mod3, g_norm1, w_in, conv_w, g_conv):
    b, s, _ = x.shape
    tm = TM_PROJ
    tok = lambda w: pl.BlockSpec((1, tm, w), lambda bi, i: (bi, i, 0))
    modspec = lambda k: pl.BlockSpec((1, 1, D_MODEL), lambda bi, i: (bi, 0, k))
    const = lambda shape: pl.BlockSpec(shape, lambda bi, i: (0,) * len(shape))
    return pl.pallas_call(
        _inproj_kernel,
        grid=(b, s // tm),
        in_specs=[tok(D_MODEL), modspec(0), modspec(1), const((1, D_MODEL)),
                  const((D_MODEL, IN_PROJ_WIDTH)), const((CONV_K, CONV_WIDTH)), const((1, CONV_WIDTH))],
        out_specs=[tok(ATTN_WIDTH), tok(2 * KV_WIDTH), tok(CONV_WIDTH)],
        out_shape=[jax.ShapeDtypeStruct((b, s, ATTN_WIDTH), BF16),
                   jax.ShapeDtypeStruct((b, s, 2 * KV_WIDTH), BF16),
                   jax.ShapeDtypeStruct((b, s, CONV_WIDTH), BF16)],
        scratch_shapes=[pltpu.VMEM((tm + SUBLANES, CONV_WIDTH), F32)],
        compiler_params=pltpu.CompilerParams(
            dimension_semantics=("arbitrary", "arbitrary"),
            vmem_limit_bytes=48 * 1024 * 1024),
        name="inproj_conv",
    )(x, mod3, mod3, g_norm1, w_in, conv_w, g_conv)


def _attn_kernel(sinks_ref, relb_ref, bucket_ref, q_ref, kv_ref, kvp_ref, cv_ref, x_ref, g1_ref,
                 ga_ref, wo_ref, o_ref, biasm, kvbuf, attn_sc, merged):
    tq = q_ref.shape[1]
    nblk = tq // BLOCK
    i = pl.program_id(1)
    n_pairs = N_Q_HEADS // 2

    @pl.when((pl.program_id(0) == 0) & (i == 0))
    def _():
        bk = bucket_ref[...]
        col = lax.broadcasted_iota(jnp.int32, (BLOCK, 2 * BLOCK), 1)
        for h in range(N_Q_HEADS):
            acc = jnp.full((BLOCK, 2 * BLOCK), NEG_INF, F32)
            for bb in range(N_BUCKETS):
                acc = jnp.where(bk == bb, relb_ref[bb, h] * LOG2E, acc)
            first = jnp.where(col < BLOCK, NEG_INF, acc)
            sink = sinks_ref[h] * LOG2E
            c0 = (h % 2) * 2 * BLOCK
            biasm[h // 2, :, c0:c0 + 2 * BLOCK] = jnp.where(col == 0, sink, acc)
            biasm[n_pairs + h // 2, :, c0:c0 + 2 * BLOCK] = jnp.where(col == 0, sink, first)

    kvbuf[0:BLOCK, :] = kvp_ref[0]
    kvbuf[BLOCK:BLOCK + tq, :] = kv_ref[0]
    lane = lax.broadcasted_iota(jnp.int32, (2 * BLOCK, LANES), 1)
    real_key = lax.broadcasted_iota(jnp.int32, (2 * BLOCK, LANES), 0) > 0
    keep_lo = (lane < HEAD_DIM) & real_key
    keep_hi = (lane >= HEAD_DIM) & real_key
    zero = jnp.zeros((2 * BLOCK, LANES), BF16)
    ones_lo = jnp.where(lane < HEAD_DIM, 1.0, 0.0).astype(BF16)
    ones_hi = jnp.where(lane >= HEAD_DIM, 1.0, 0.0).astype(BF16)

    def block(j):
        r0 = j * BLOCK
        table = jnp.where(i == 0, n_pairs, 0) if j == 0 else 0
        qblk = q_ref[0, r0:r0 + BLOCK, :]
        kvw = kvbuf[r0:r0 + 2 * BLOCK, :]
        k = kvw[:, :LANES]
        v = kvw[:, LANES:]
        kr = pltpu.roll(k, HEAD_DIM, 1)
        vr = pltpu.roll(v, HEAD_DIM, 1)
        osq = jnp.zeros((BLOCK, LANES), F32)
        for kh in range(N_KV_HEADS):
            k_lo, k_hi = (k, kr) if kh == 0 else (kr, k)
            v_lo, v_hi = (v, vr) if kh == 0 else (vr, v)
            kx = jnp.concatenate([jnp.where(keep_lo, k_lo, zero), jnp.where(keep_hi, k_hi, zero)], axis=0)
            vx = jnp.concatenate(
                [jnp.concatenate([jnp.where(keep_lo, v_lo, zero), ones_lo], axis=1),
                 jnp.concatenate([jnp.where(keep_hi, v_hi, zero), ones_hi], axis=1)], axis=0)
            for pp in range(2):
                pair = kh * 2 + pp
                qp = qblk[:, pair * LANES:(pair + 1) * LANES]
                sc = lax.dot_general(qp, kx, (((1,), (1,)), ((), ())), preferred_element_type=F32)
                sc = sc + biasm[table + pair]
                es = []
                for half in range(2):
                    st = sc[:, half * 2 * BLOCK:(half + 1) * 2 * BLOCK]
                    m = jnp.max(st, axis=-1, keepdims=True)
                    es.append(jnp.exp2(st - m).astype(BF16))
                pv = jnp.dot(jnp.concatenate(es, axis=1), vx, preferred_element_type=F32)
                o = pv[:, :LANES] * (1.0 / pv[:, LANES:])
                attn_sc[:, pair * LANES:(pair + 1) * LANES] = o
                osq = osq + o * o
        scale = lax.rsqrt(jnp.sum(osq, axis=-1, keepdims=True) * (1.0 / ATTN_WIDTH) + EPS)
        merged[r0:r0 + BLOCK, 0:ATTN_WIDTH] = (attn_sc[...] * scale * ga_ref[...]).astype(BF16)

    merged[:, ATTN_WIDTH:] = cv_ref[0]
    blocks_per_chunk = OUTPROJ_ROWS // BLOCK
    for j in range(nblk):
        block(j)
        if (j + 1) % blocks_per_chunk == 0:
            rows = slice((j + 1 - blocks_per_chunk) * BLOCK, (j + 1) * BLOCK)
            y = jnp.dot(merged[rows, :], wo_ref[...], preferred_element_type=F32)
            o_ref[0, rows, :] = x_ref[0, rows, :] + g1_ref[0] * y


def _attn_outproj(q, kv, convn, x, mod3, sinks, rel_bias, g_attn, w_out):
    b, s, _ = x.shape
    tq = TQ_ATTN
    nblk = tq // BLOCK
    tok = lambda w: pl.BlockSpec((1, tq, w), lambda bi, i: (bi, i, 0))
    const = lambda shape: pl.BlockSpec(shape, lambda bi, i: (0,) * len(shape))
    smem = pl.BlockSpec(memory_space=pltpu.SMEM)
    bucket = jnp.asarray(_bucket_table())
    return pl.pallas_call(
        _attn_kernel,
        grid=(b, s // tq),
        in_specs=[smem, smem, const((BLOCK, 2 * BLOCK)),
                  tok(ATTN_WIDTH), tok(2 * KV_WIDTH),
                  pl.BlockSpec((1, BLOCK, 2 * KV_WIDTH),
                               lambda bi, i: (bi, jnp.maximum(i * nblk - 1, 0), 0)),
                  tok(CONV_WIDTH), tok(D_MODEL),
                  pl.BlockSpec((1, 1, D_MODEL), lambda bi, i: (bi, 0, 2)),
                  const((1, ATTN_WIDTH)), const((D_MODEL, D_MODEL))],
        out_specs=tok(D_MODEL),
        out_shape=jax.ShapeDtypeStruct((b, s, D_MODEL), F32),
        scratch_shapes=[pltpu.VMEM((N_Q_HEADS, BLOCK, 4 * BLOCK), F32),
                        pltpu.VMEM((tq + BLOCK, 2 * KV_WIDTH), BF16),
                        pltpu.VMEM((BLOCK, ATTN_WIDTH), F32),
                        pltpu.VMEM((tq, D_MODEL), BF16)],
        compiler_params=pltpu.CompilerParams(
            dimension_semantics=("arbitrary", "arbitrary"),
            vmem_limit_bytes=48 * 1024 * 1024),
        name="attn_outproj",
    )(sinks, rel_bias, bucket, q, kv, kv, convn, x, mod3, g_attn, w_out)


def _ffn_kernel(x_ref, sh_ref, sc_ref, g2_ref, gn_ref, wgu_ref, wd_ref, gf_ref, o_ref, act, *, final_norm):
    gain = gn_ref[...] * (1.0 + sc_ref[0])
    shift = sh_ref[0]
    subs = [slice(r0, r0 + FFN_SUB) for r0 in range(0, x_ref.shape[1], FFN_SUB)]
    hs = []
    for rows in subs:
        x = x_ref[0, rows, :]
        hs.append(((x * _rms_scale(x, D_MODEL)) * gain + shift).astype(BF16))
    for rows, h in zip(subs, hs):
        for c0 in range(0, D_FF, FF_CHUNK):
            c1 = min(c0 + FF_CHUNK, D_FF)
            gate = jnp.dot(h, wgu_ref[:, c0:c1], preferred_element_type=F32)
            up = jnp.dot(h, wgu_ref[:, D_FF + c0:D_FF + c1], preferred_element_type=F32)
            act[rows, c0:c1] = (gate * jax.nn.sigmoid(gate) * up).astype(BF16)
        y = jnp.dot(act[rows, :], wd_ref[...], preferred_element_type=F32)
        x2 = x_ref[0, rows, :] + g2_ref[0] * y
        o_ref[0, rows, :] = x2 * _rms_scale(x2, D_MODEL) * gf_ref[...] if final_norm else x2


def _ffn(x1, mod3, g_norm2, w_gu, w_down, g_final, final_norm):
    b, s, _ = x1.shape
    tm = TM_FFN
    tok = pl.BlockSpec((1, tm, D_MODEL), lambda bi, i: (bi, i, 0))
    modspec = lambda k: pl.BlockSpec((1, 1, D_MODEL), lambda bi, i: (bi, 0, k))
    const = lambda shape: pl.BlockSpec(shape, lambda bi, i: (0,) * len(shape))
    resident = lambda shape: pl.BlockSpec(shape, lambda bi, i: (0,) * len(shape),
                                          pipeline_mode=pl.Buffered(1))
    return pl.pallas_call(
        functools.partial(_ffn_kernel, final_norm=final_norm),
        grid=(b, s // tm),
        in_specs=[tok, modspec(3), modspec(4), modspec(5), const((1, D_MODEL)),
                  resident((D_MODEL, 2 * D_FF)), resident((D_FF, D_MODEL)), const((1, D_MODEL))],
        out_specs=tok,
        out_shape=jax.ShapeDtypeStruct((b, s, D_MODEL), F32),
        scratch_shapes=[pltpu.VMEM((tm, D_FF), BF16)],
        compiler_params=pltpu.CompilerParams(
            dimension_semantics=("parallel", "parallel"),
            vmem_limit_bytes=56 * 1024 * 1024),
        name="ffn_final",
    )(x1, mod3, mod3, mod3, g_norm2, w_gu, w_down, g_final)


def kernel(x, c, rel_bias, w_ada, b_ada, g_norm1, w_in, sinks, conv_w, g_attn_out, g_conv_out,
           w_out, g_norm2, w_gu, w_down, g_final):
    depth = w_ada.shape[0]
    b = x.shape[0]
    for l in range(depth):
        mod3 = _adaln(c, w_ada[l], b_ada[l]).reshape(b, 1, N_MOD * D_MODEL)
        q, kv, convn = _inproj(x, mod3, g_norm1[l].reshape(1, D_MODEL), w_in[l].astype(BF16),
                               conv_w[l], g_conv_out[l].reshape(1, CONV_WIDTH))
        x1 = _attn_outproj(q, kv, convn, x, mod3, sinks[l], rel_bias,
                           g_attn_out[l].reshape(1, ATTN_WIDTH), w_out[l].astype(BF16))
        x = _ffn(x1, mod3, g_norm2[l].reshape(1, D_MODEL), w_gu[l].astype(BF16),
                 w_down[l].astype(BF16), g_final.reshape(1, D_MODEL), final_norm=(l == depth - 1))
    return x
```

```python
import functools
import math

import jax
import jax.numpy as jnp
import numpy as np
from jax import lax
from jax.experimental import pallas as pl
from jax.experimental.pallas import tpu as pltpu

D_MODEL = 1024
HEAD_DIM = 64
N_Q_HEADS = 8
N_KV_HEADS = 2
ATTN_WIDTH = N_Q_HEADS * HEAD_DIM
KV_WIDTH = N_KV_HEADS * HEAD_DIM
WINDOW = 128
BLOCK = 128
N_BUCKETS = 32
MAX_DISTANCE = 128
CONV_WIDTH = D_MODEL - ATTN_WIDTH
CONV_K = 3
IN_PROJ_WIDTH = ATTN_WIDTH + 2 * KV_WIDTH + 3 * CONV_WIDTH
D_FF = -(-8 * D_MODEL // (3 * 256)) * 256
N_MOD = 6
EPS = 1e-6
NEG_INF = -1e30
LOG2E = math.log2(math.e)

LANES = 128
SUBLANES = 8
MXU_DIM = 256
VMEM_BYTES = 64 * 1024 * 1024

F32 = jnp.float32
BF16 = jnp.bfloat16

TM_PROJ = 1024
PROJ_SUB = 256
TQ_ATTN = 512
OUTPROJ_ROWS = 256
TM_FFN = 512
FFN_SUB = 256
ADA_TN = 1024
FF_CHUNK = 2 * MXU_DIM


def _bucket_table():
    qi = np.arange(BLOCK, dtype=np.int32)[:, None]
    sj = np.arange(2 * BLOCK, dtype=np.int32)[None, :]
    dist = qi + BLOCK - sj
    d0 = np.maximum(dist, 0)
    max_exact = N_BUCKETS // 2
    d = np.maximum(d0, 1).astype(np.float32)
    large = max_exact + (np.log(d / np.float32(max_exact)) / np.float32(math.log(MAX_DISTANCE / max_exact))
                         * np.float32(N_BUCKETS - max_exact)).astype(np.int32)
    large = np.minimum(large, N_BUCKETS - 1)
    bucket = np.where(d0 < max_exact, d0, large)
    in_window = (dist >= 0) & (dist < WINDOW)
    return np.where(in_window, bucket, -1).astype(np.int32)


def _rms_scale(v, width):
    return lax.rsqrt(jnp.sum(v * v, axis=-1, keepdims=True) * (1.0 / width) + EPS)


def _adaln_kernel(c_ref, w_ref, b_ref, o_ref):
    c = c_ref[...]
    cond = c * jax.nn.sigmoid(c)
    o_ref[...] = jnp.dot(cond.astype(BF16), w_ref[...].astype(BF16),
                         preferred_element_type=F32) + b_ref[...]


def _adaln(c, w_ada, b_ada):
    b = c.shape[0]
    n = w_ada.shape[1]
    return pl.pallas_call(
        _adaln_kernel,
        grid=(n // ADA_TN,),
        in_specs=[pl.BlockSpec((b, D_MODEL), lambda j: (0, 0)),
                  pl.BlockSpec((D_MODEL, ADA_TN), lambda j: (0, j)),
                  pl.BlockSpec((1, ADA_TN), lambda j: (0, j))],
        out_specs=pl.BlockSpec((b, ADA_TN), lambda j: (0, j)),
        out_shape=jax.ShapeDtypeStruct((b, n), F32),
        name="adaln_mod",
    )(c, w_ada, b_ada.reshape(1, n))


def _inproj_kernel(x_ref, sh_ref, sc_ref, gn_ref, w_ref, cw_ref, gc_ref,
                   q_ref, kv_ref, cv_ref, ubuf):
    tm = x_ref.shape[1]

    @pl.when(pl.program_id(1) == 0)
    def _():
        ubuf[0:SUBLANES, :] = jnp.zeros((SUBLANES, CONV_WIDTH), F32)

    gain = gn_ref[...] * (1.0 + sc_ref[0])
    shift = sh_ref[0]
    subs = [slice(r0, r0 + PROJ_SUB) for r0 in range(0, tm, PROJ_SUB)]
    hs = []
    for rows in subs:
        x = x_ref[0, rows, :]
        hs.append(((x * _rms_scale(x, D_MODEL)) * gain + shift).astype(BF16))
    c0 = ATTN_WIDTH + 2 * KV_WIDTH
    for rows, h in zip(subs, hs):
        proj = jnp.dot(h, w_ref[...], preferred_element_type=F32)
        q_ref[0, rows, :] = (proj[:, :ATTN_WIDTH] * (HEAD_DIM ** -0.5 * LOG2E)).astype(BF16)
        kv_ref[0, rows, :] = proj[:, ATTN_WIDTH:c0].astype(BF16)
        gate_b = proj[:, c0:c0 + CONV_WIDTH]
        u = proj[:, c0 + CONV_WIDTH:c0 + 2 * CONV_WIDTH] * proj[:, c0 + 2 * CONV_WIDTH:]
        r0 = rows.start + SUBLANES
        ubuf[r0:r0 + PROJ_SUB, :] = u
        u1 = ubuf[r0 - 1:r0 - 1 + PROJ_SUB, :]
        u2 = ubuf[r0 - 2:r0 - 2 + PROJ_SUB, :]
        conv = gate_b * (cw_ref[0:1, :] * u2 + cw_ref[1:2, :] * u1 + cw_ref[2:3, :] * u)
        cv_ref[0, rows, :] = (conv * _rms_scale(conv, CONV_WIDTH) * gc_ref[...]).astype(BF16)
    ubuf[0:SUBLANES, :] = ubuf[tm:tm + SUBLANES, :]


def _inproj(x, mod3, g_norm1, w_in, conv_w, g_conv):
    b, s, _ = x.shape
    tm = TM_PROJ
    tok = lambda w: pl.BlockSpec((1, tm, w), lambda bi, i: (bi, i, 0))
    modspec = lambda k: pl.BlockSpec((1, 1, D_MODEL), lambda bi, i: (bi, 0, k))
    const = lambda shape: pl.BlockSpec(shape, lambda bi, i: (0,) * len(shape))
    return pl.pallas_call(
        _inproj_kernel,
        grid=(b, s // tm),
        in_specs=[tok(D_MODEL), modspec(0), modspec(1), const((1, D_MODEL)),
                  const((D_MODEL, IN_PROJ_WIDTH)), const((CONV_K, CONV_WIDTH)), const((1, CONV_WIDTH))],
        out_specs=[tok(ATTN_WIDTH), tok(2 * KV_WIDTH), tok(CONV_WIDTH)],
        out_shape=[jax.ShapeDtypeStruct((b, s, ATTN_WIDTH), BF16),
                   jax.ShapeDtypeStruct((b, s, 2 * KV_WIDTH), BF16),
                   jax.ShapeDtypeStruct((b, s, CONV_WIDTH), BF16)],
        scratch_shapes=[pltpu.VMEM((tm + SUBLANES, CONV_WIDTH), F32)],
        compiler_params=pltpu.CompilerParams(
            dimension_semantics=("arbitrary", "arbitrary"),
            vmem_limit_bytes=48 * 1024 * 1024),
        name="inproj_conv",
    )(x, mod3, mod3, g_norm1, w_in, conv_w, g_conv)


def _mixer_ffn_kernel(sinks_ref, relb_ref, bucket_ref, q_ref, kv_ref, kvp_ref, cv_ref, x_ref, g1_ref,
                      sh2_ref, sc2_ref, g2_ref, ga_ref, wo_ref, gn2_ref, wgu_ref, wd_ref, gf_ref, o_ref,
                      biasm, kvbuf, attn_sc, merged, x1_sc, x1_res, hbuf, act,
                      *, n_tiles, tiles_per_seq, final_norm):
    tq = q_ref.shape[1]
    nblk = tq // BLOCK
    t = pl.program_id(0)
    i = jnp.minimum(t, n_tiles - 1) % tiles_per_seq
    n_pairs = N_Q_HEADS // 2

    @pl.when(t == 0)
    def _():
        x1_sc[...] = jnp.zeros(x1_sc.shape, F32)
        bk = bucket_ref[...]
        col = lax.broadcasted_iota(jnp.int32, (BLOCK, 2 * BLOCK), 1)
        for h in range(N_Q_HEADS):
            acc = jnp.full((BLOCK, 2 * BLOCK), NEG_INF, F32)
            for bb in range(N_BUCKETS):
                acc = jnp.where(bk == bb, relb_ref[bb, h] * LOG2E, acc)
            first = jnp.where(col < BLOCK, NEG_INF, acc)
            sink = sinks_ref[h] * LOG2E
            c0 = (h % 2) * 2 * BLOCK
            biasm[h // 2, :, c0:c0 + 2 * BLOCK] = jnp.where(col == 0, sink, acc)
            biasm[n_pairs + h // 2, :, c0:c0 + 2 * BLOCK] = jnp.where(col == 0, sink, first)

    gain = gn2_ref[...] * (1.0 + sc2_ref[0])
    shift = sh2_ref[0]
    subs = [slice(r0, r0 + FFN_SUB) for r0 in range(0, tq, FFN_SUB)]

    def ffn_norm(rows):
        x1 = x1_sc[rows, :]
        x1_res[rows, :] = x1
        hbuf[rows, :] = ((x1 * _rms_scale(x1, D_MODEL)) * gain + shift).astype(BF16)

    def ffn_chunk(rows, c0):
        c1 = min(c0 + FF_CHUNK, D_FF)
        h = hbuf[rows, :]
        gate = jnp.dot(h, wgu_ref[:, c0:c1], preferred_element_type=F32)
        up = jnp.dot(h, wgu_ref[:, D_FF + c0:D_FF + c1], preferred_element_type=F32)
        act[rows, c0:c1] = (gate * jax.nn.sigmoid(gate) * up).astype(BF16)

    def ffn_down(rows):
        y = jnp.dot(act[rows, :], wd_ref[...], preferred_element_type=F32)
        x2 = x1_res[rows, :] + g2_ref[0] * y
        o_ref[0, rows, :] = x2 * _rms_scale(x2, D_MODEL) * gf_ref[...] if final_norm else x2

    ffn_units = []
    for rows in subs:
        ffn_units += [functools.partial(ffn_chunk, rows, c0) for c0 in range(0, D_FF, FF_CHUNK)]
        ffn_units.append(functools.partial(ffn_down, rows))

    lane = lax.broadcasted_iota(jnp.int32, (2 * BLOCK, LANES), 1)
    real_key = lax.broadcasted_iota(jnp.int32, (2 * BLOCK, LANES), 0) > 0
    keep_lo = (lane < HEAD_DIM) & real_key
    keep_hi = (lane >= HEAD_DIM) & real_key
    zero = jnp.zeros((2 * BLOCK, LANES), BF16)
    ones_lo = jnp.where(lane < HEAD_DIM, 1.0, 0.0).astype(BF16)
    ones_hi = jnp.where(lane >= HEAD_DIM, 1.0, 0.0).astype(BF16)

    def attn_group(j, kh):
        r0 = j * BLOCK
        table = jnp.where(i == 0, n_pairs, 0) if j == 0 else 0
        kvw = kvbuf[r0:r0 + 2 * BLOCK, :]
        k = kvw[:, :LANES]
        v = kvw[:, LANES:]
        kr = pltpu.roll(k, HEAD_DIM, 1)
        vr = pltpu.roll(v, HEAD_DIM, 1)
        k_lo, k_hi = (k, kr) if kh == 0 else (kr, k)
        v_lo, v_hi = (v, vr) if kh == 0 else (vr, v)
        kx = jnp.concatenate([jnp.where(keep_lo, k_lo, zero), jnp.where(keep_hi, k_hi, zero)], axis=0)
        vx = jnp.concatenate(
            [jnp.concatenate([jnp.where(keep_lo, v_lo, zero), ones_lo], axis=1),
             jnp.concatenate([jnp.where(keep_hi, v_hi, zero), ones_hi], axis=1)], axis=0)
        for pp in range(2):
            pair = kh * 2 + pp
            qp = q_ref[0, r0:r0 + BLOCK, pair * LANES:(pair + 1) * LANES]
            sc = lax.dot_general(qp, kx, (((1,), (1,)), ((), ())), preferred_element_type=F32)
            sc = sc + biasm[table + pair]
            es = []
            for half in range(2):
                st = sc[:, half * 2 * BLOCK:(half + 1) * 2 * BLOCK]
                m = jnp.max(st, axis=-1, keepdims=True)
                es.append(jnp.exp2(st - m).astype(BF16))
            pv = jnp.dot(jnp.concatenate(es, axis=1), vx, preferred_element_type=F32)
            attn_sc[r0:r0 + BLOCK, pair * LANES:(pair + 1) * LANES] = pv[:, :LANES] * (1.0 / pv[:, LANES:])

    def attn_finish(j):
        rows = slice(j * BLOCK, (j + 1) * BLOCK)
        a = attn_sc[rows, :]
        merged[rows, 0:ATTN_WIDTH] = (a * _rms_scale(a, ATTN_WIDTH) * ga_ref[...]).astype(BF16)

    def outproj(rows):
        y = jnp.dot(merged[rows, :], wo_ref[...], preferred_element_type=F32)
        x1_sc[rows, :] = x_ref[0, rows, :] + g1_ref[0] * y

    blocks_per_chunk = OUTPROJ_ROWS // BLOCK
    attn_units = []
    for j in range(nblk):
        attn_units.append(functools.partial(attn_group, j, 0))
        tail = [functools.partial(attn_group, j, 1), functools.partial(attn_finish, j)]
        if (j + 1) % blocks_per_chunk == 0:
            tail.append(functools.partial(outproj, slice((j + 1 - blocks_per_chunk) * BLOCK, (j + 1) * BLOCK)))
        attn_units.append(lambda fs=tail: [f() for f in fs])

    for rows in subs:
        ffn_norm(rows)
    kvbuf[0:BLOCK, :] = kvp_ref[0]
    kvbuf[BLOCK:BLOCK + tq, :] = kv_ref[0]
    merged[:, ATTN_WIDTH:] = cv_ref[0]
    done = 0
    for n, unit in enumerate(ffn_units):
        unit()
        want = (n + 1) * len(attn_units) // len(ffn_units)
        for a in attn_units[done:want]:
            a()
        done = want


def _mixer_ffn(q, kv, convn, x, mod3, sinks, rel_bias, g_attn, w_out, g_norm2, w_gu, w_down, g_final,
               final_norm):
    b, s, _ = x.shape
    tq = TQ_ATTN
    nblk = tq // BLOCK
    tps = s // tq
    n_tiles = b * tps
    att = lambda t: (jnp.minimum(t, n_tiles - 1) // tps, jnp.minimum(t, n_tiles - 1) % tps)
    ffn = lambda t: (jnp.maximum(t - 1, 0) // tps, jnp.maximum(t - 1, 0) % tps)
    att_tok = lambda w: pl.BlockSpec((1, tq, w), lambda t: (*att(t), 0))
    att_mod = lambda k: pl.BlockSpec((1, 1, D_MODEL), lambda t: (att(t)[0], 0, k))
    ffn_mod = lambda k: pl.BlockSpec((1, 1, D_MODEL), lambda t: (ffn(t)[0], 0, k))
    const = lambda shape: pl.BlockSpec(shape, lambda t: (0,) * len(shape))
    resident = lambda shape: pl.BlockSpec(shape, lambda t: (0,) * len(shape), pipeline_mode=pl.Buffered(1))
    smem = pl.BlockSpec(memory_space=pltpu.SMEM)
    bucket = jnp.asarray(_bucket_table())
    return pl.pallas_call(
        functools.partial(_mixer_ffn_kernel, n_tiles=n_tiles, tiles_per_seq=tps, final_norm=final_norm),
        grid=(n_tiles + 1,),
        in_specs=[smem, smem, const((BLOCK, 2 * BLOCK)),
                  att_tok(ATTN_WIDTH), att_tok(2 * KV_WIDTH),
                  pl.BlockSpec((1, BLOCK, 2 * KV_WIDTH),
                               lambda t: (att(t)[0], jnp.maximum(att(t)[1] * nblk - 1, 0), 0)),
                  att_tok(CONV_WIDTH), att_tok(D_MODEL), att_mod(2),
                  ffn_mod(3), ffn_mod(4), ffn_mod(5),
                  const((1, ATTN_WIDTH)), resident((D_MODEL, D_MODEL)), const((1, D_MODEL)),
                  resident((D_MODEL, 2 * D_FF)), resident((D_FF, D_MODEL)), const((1, D_MODEL))],
        out_specs=pl.BlockSpec((1, tq, D_MODEL), lambda t: (*ffn(t), 0)),
        out_shape=jax.ShapeDtypeStruct((b, s, D_MODEL), F32),
        scratch_shapes=[pltpu.VMEM((N_Q_HEADS, BLOCK, 4 * BLOCK), F32),
                        pltpu.VMEM((tq + BLOCK, 2 * KV_WIDTH), BF16),
                        pltpu.VMEM((tq, ATTN_WIDTH), F32),
                        pltpu.VMEM((tq, D_MODEL), BF16),
                        pltpu.VMEM((tq, D_MODEL), F32),
                        pltpu.VMEM((tq, D_MODEL), F32),
                        pltpu.VMEM((tq, D_MODEL), BF16),
                        pltpu.VMEM((tq, D_FF), BF16)],
        compiler_params=pltpu.CompilerParams(
            dimension_semantics=("arbitrary",),
            vmem_limit_bytes=56 * 1024 * 1024),
        name="mixer_ffn",
    )(sinks, rel_bias, bucket, q, kv, kv, convn, x, mod3, mod3, mod3, mod3, g_attn, w_out, g_norm2,
      w_gu, w_down, g_final)


def kernel(x, c, rel_bias, w_ada, b_ada, g_norm1, w_in, sinks, conv_w, g_attn_out, g_conv_out,
           w_out, g_norm2, w_gu, w_down, g_final):
    depth = w_ada.shape[0]
    b = x.shape[0]
    for l in range(depth):
        mod3 = _adaln(c, w_ada[l], b_ada[l]).reshape(b, 1, N_MOD * D_MODEL)
        q, kv, convn = _inproj(x, mod3, g_norm1[l].reshape(1, D_MODEL), w_in[l].astype(BF16),
                               conv_w[l], g_conv_out[l].reshape(1, CONV_WIDTH))
        x = _mixer_ffn(q, kv, convn, x, mod3, sinks[l], rel_bias, g_attn_out[l].reshape(1, ATTN_WIDTH),
                       w_out[l].astype(BF16), g_norm2[l].reshape(1, D_MODEL), w_gu[l].astype(BF16),
                       w_down[l].astype(BF16), g_final.reshape(1, D_MODEL), final_norm=(l == depth - 1))
    return x
```

```python
import functools
import math

import jax
import jax.numpy as jnp
import numpy as np
from jax import lax
from jax.experimental import pallas as pl
from jax.experimental.pallas import tpu as pltpu

D_MODEL = 1024
HEAD_DIM = 64
N_Q_HEADS = 8
N_KV_HEADS = 2
ATTN_WIDTH = N_Q_HEADS * HEAD_DIM
KV_WIDTH = N_KV_HEADS * HEAD_DIM
WINDOW = 128
BLOCK = 128
N_BUCKETS = 32
MAX_DISTANCE = 128
CONV_WIDTH = D_MODEL - ATTN_WIDTH
CONV_K = 3
IN_PROJ_WIDTH = ATTN_WIDTH + 2 * KV_WIDTH + 3 * CONV_WIDTH
D_FF = -(-8 * D_MODEL // (3 * 256)) * 256
N_MOD = 6
EPS = 1e-6
NEG_INF = -1e30
LOG2E = math.log2(math.e)

LANES = 128
SUBLANES = 8
MXU_DIM = 256
VMEM_BYTES = 64 * 1024 * 1024

F32 = jnp.float32
BF16 = jnp.bfloat16

TM_PROJ = 1024
PROJ_SUB = 256
TQ_ATTN = 512
OUTPROJ_ROWS = 256
TM_FFN = 512
FFN_SUB = 256
ADA_TN = 1024
FF_CHUNK = 2 * MXU_DIM


def _bucket_table():
    qi = np.arange(BLOCK, dtype=np.int32)[:, None]
    sj = np.arange(2 * BLOCK, dtype=np.int32)[None, :]
    dist = qi + BLOCK - sj
    d0 = np.maximum(dist, 0)
    max_exact = N_BUCKETS // 2
    d = np.maximum(d0, 1).astype(np.float32)
    large = max_exact + (np.log(d / np.float32(max_exact)) / np.float32(math.log(MAX_DISTANCE / max_exact))
                         * np.float32(N_BUCKETS - max_exact)).astype(np.int32)
    large = np.minimum(large, N_BUCKETS - 1)
    bucket = np.where(d0 < max_exact, d0, large)
    in_window = (dist >= 0) & (dist < WINDOW)
    return np.where(in_window, bucket, -1).astype(np.int32)


def _rms_scale(v, width):
    return lax.rsqrt(jnp.sum(v * v, axis=-1, keepdims=True) * (1.0 / width) + EPS)


def _adaln_kernel(c_ref, w_ref, b_ref, o_ref):
    c = c_ref[...]
    cond = c * jax.nn.sigmoid(c)
    o_ref[...] = jnp.dot(cond.astype(BF16), w_ref[...].astype(BF16),
                         preferred_element_type=F32) + b_ref[...]


def _adaln(c, w_ada, b_ada):
    b = c.shape[0]
    n = w_ada.shape[1]
    return pl.pallas_call(
        _adaln_kernel,
        grid=(n // ADA_TN,),
        in_specs=[pl.BlockSpec((b, D_MODEL), lambda j: (0, 0)),
                  pl.BlockSpec((D_MODEL, ADA_TN), lambda j: (0, j)),
                  pl.BlockSpec((1, ADA_TN), lambda j: (0, j))],
        out_specs=pl.BlockSpec((b, ADA_TN), lambda j: (0, j)),
        out_shape=jax.ShapeDtypeStruct((b, n), F32),
        name="adaln_mod",
    )(c, w_ada, b_ada.reshape(1, n))


def _inproj_kernel(x_ref, sh_ref, sc_ref, gn_ref, w_ref, cw_ref, gc_ref, wo_ref, wgu_ref, wd_ref,
                   q_ref, kv_ref, cv_ref, wo_bf_ref, wgu_bf_ref, wd_bf_ref, ubuf):
    tm = x_ref.shape[1]

    wo_bf_ref[...] = wo_ref[...].astype(BF16)
    wgu_bf_ref[...] = wgu_ref[...].astype(BF16)
    wd_bf_ref[...] = wd_ref[...].astype(BF16)

    @pl.when(pl.program_id(1) == 0)
    def _():
        ubuf[0:SUBLANES, :] = jnp.zeros((SUBLANES, CONV_WIDTH), F32)

    gain = gn_ref[...] * (1.0 + sc_ref[0])
    shift = sh_ref[0]
    subs = [slice(r0, r0 + PROJ_SUB) for r0 in range(0, tm, PROJ_SUB)]
    hs = []
    for rows in subs:
        x = x_ref[0, rows, :]
        hs.append(((x * _rms_scale(x, D_MODEL)) * gain + shift).astype(BF16))
    c0 = ATTN_WIDTH + 2 * KV_WIDTH
    for rows, h in zip(subs, hs):
        proj = jnp.dot(h, w_ref[...], preferred_element_type=F32)
        q_ref[0, rows, :] = (proj[:, :ATTN_WIDTH] * (HEAD_DIM ** -0.5 * LOG2E)).astype(BF16)
        kv_ref[0, rows, :] = proj[:, ATTN_WIDTH:c0].astype(BF16)
        gate_b = proj[:, c0:c0 + CONV_WIDTH]
        u = proj[:, c0 + CONV_WIDTH:c0 + 2 * CONV_WIDTH] * proj[:, c0 + 2 * CONV_WIDTH:]
        r0 = rows.start + SUBLANES
        ubuf[r0:r0 + PROJ_SUB, :] = u
        ue = ubuf[r0 - SUBLANES:r0 + PROJ_SUB, :]
        u1 = pltpu.roll(ue, 1, 0)[SUBLANES:, :]
        u2 = pltpu.roll(ue, 2, 0)[SUBLANES:, :]
        conv = gate_b * (cw_ref[0:1, :] * u2 + cw_ref[1:2, :] * u1 + cw_ref[2:3, :] * u)
        cv_ref[0, rows, :] = (conv * _rms_scale(conv, CONV_WIDTH) * gc_ref[...]).astype(BF16)
    ubuf[0:SUBLANES, :] = ubuf[tm:tm + SUBLANES, :]


def _inproj(x, mod3, g_norm1, w_in, conv_w, g_conv, w_out, w_gu, w_down):
    b, s, _ = x.shape
    tm = TM_PROJ
    tps = s // tm
    n_steps = b * tps
    tok = lambda w: pl.BlockSpec((1, tm, w), lambda bi, i: (bi, i, 0))
    modspec = lambda k: pl.BlockSpec((1, 1, D_MODEL), lambda bi, i: (bi, 0, k))
    const = lambda shape: pl.BlockSpec(shape, lambda bi, i: (0,) * len(shape))
    slabs = [w.reshape(n_steps, w.shape[0] // n_steps, w.shape[1]) for w in (w_out, w_gu, w_down)]
    slab = lambda w: pl.BlockSpec((1,) + w.shape[1:], lambda bi, i: (bi * tps + i, 0, 0))
    outs = pl.pallas_call(
        _inproj_kernel,
        grid=(b, tps),
        in_specs=[tok(D_MODEL), modspec(0), modspec(1), const((1, D_MODEL)),
                  const((D_MODEL, IN_PROJ_WIDTH)), const((CONV_K, CONV_WIDTH)), const((1, CONV_WIDTH)),
                  *[slab(w) for w in slabs]],
        out_specs=[tok(ATTN_WIDTH), tok(2 * KV_WIDTH), tok(CONV_WIDTH), *[slab(w) for w in slabs]],
        out_shape=[jax.ShapeDtypeStruct((b, s, ATTN_WIDTH), BF16),
                   jax.ShapeDtypeStruct((b, s, 2 * KV_WIDTH), BF16),
                   jax.ShapeDtypeStruct((b, s, CONV_WIDTH), BF16),
                   *[jax.ShapeDtypeStruct(w.shape, BF16) for w in slabs]],
        scratch_shapes=[pltpu.VMEM((tm + SUBLANES, CONV_WIDTH), F32)],
        compiler_params=pltpu.CompilerParams(
            dimension_semantics=("arbitrary", "arbitrary"),
            vmem_limit_bytes=48 * 1024 * 1024),
        name="inproj_conv",
    )(x, mod3, mod3, g_norm1, w_in, conv_w, g_conv, *slabs)
    q, kv, convn = outs[:3]
    return q, kv, convn, *[o.reshape(w.shape) for o, w in zip(outs[3:], (w_out, w_gu, w_down))]


def _mixer_ffn_kernel(sinks_ref, relb_ref, bucket_ref, q_ref, kv_ref, kvp_ref, cv_ref, x_ref, g1_ref,
                      sh2_ref, sc2_ref, g2_ref, ga_ref, wo_ref, gn2_ref, wgu_ref, wd_ref, gf_ref, o_ref,
                      biasm, kvbuf, attn_sc, merged, x1_sc, x1_res, hbuf, act,
                      *, n_tiles, tiles_per_seq, final_norm):
    tq = q_ref.shape[1]
    nblk = tq // BLOCK
    t = pl.program_id(0)
    i = jnp.minimum(t, n_tiles - 1) % tiles_per_seq
    n_pairs = N_Q_HEADS // 2

    @pl.when(t == 0)
    def _():
        x1_sc[...] = jnp.zeros(x1_sc.shape, F32)
        bk = bucket_ref[...]
        col = lax.broadcasted_iota(jnp.int32, (BLOCK, 2 * BLOCK), 1)
        for h in range(N_Q_HEADS):
            acc = jnp.full((BLOCK, 2 * BLOCK), NEG_INF, F32)
            for bb in range(N_BUCKETS):
                acc = jnp.where(bk == bb, relb_ref[bb, h] * LOG2E, acc)
            first = jnp.where(col < BLOCK, NEG_INF, acc)
            sink = sinks_ref[h] * LOG2E
            c0 = (h % 2) * 2 * BLOCK
            biasm[h // 2, :, c0:c0 + 2 * BLOCK] = jnp.where(col == 0, sink, acc)
            biasm[n_pairs + h // 2, :, c0:c0 + 2 * BLOCK] = jnp.where(col == 0, sink, first)

    gain = gn2_ref[...] * (1.0 + sc2_ref[0])
    shift = sh2_ref[0]
    subs = [slice(r0, r0 + FFN_SUB) for r0 in range(0, tq, FFN_SUB)]

    def ffn_norm(rows):
        x1 = x1_sc[rows, :]
        x1_res[rows, :] = x1
        hbuf[rows, :] = ((x1 * _rms_scale(x1, D_MODEL)) * gain + shift).astype(BF16)

    def ffn_chunk(rows, c0):
        c1 = min(c0 + FF_CHUNK, D_FF)
        h = hbuf[rows, :]
        gate = jnp.dot(h, wgu_ref[:, c0:c1], preferred_element_type=F32)
        up = jnp.dot(h, wgu_ref[:, D_FF + c0:D_FF + c1], preferred_element_type=F32)
        act[rows, c0:c1] = (gate * jax.nn.sigmoid(gate) * up).astype(BF16)

    def ffn_down(rows):
        y = jnp.dot(act[rows, :], wd_ref[...], preferred_element_type=F32)
        x2 = x1_res[rows, :] + g2_ref[0] * y
        o_ref[0, rows, :] = x2 * _rms_scale(x2, D_MODEL) * gf_ref[...] if final_norm else x2

    ffn_units = []
    for rows in subs:
        ffn_units += [functools.partial(ffn_chunk, rows, c0) for c0 in range(0, D_FF, FF_CHUNK)]
        ffn_units.append(functools.partial(ffn_down, rows))

    lane = lax.broadcasted_iota(jnp.int32, (2 * BLOCK, LANES), 1)
    real_key = lax.broadcasted_iota(jnp.int32, (2 * BLOCK, LANES), 0) > 0
    keep_lo = (lane < HEAD_DIM) & real_key
    keep_hi = (lane >= HEAD_DIM) & real_key
    zero = jnp.zeros((2 * BLOCK, LANES), BF16)
    ones_lo = jnp.where(lane < HEAD_DIM, 1.0, 0.0).astype(BF16)
    ones_hi = jnp.where(lane >= HEAD_DIM, 1.0, 0.0).astype(BF16)

    def attn_group(j, kh):
        r0 = j * BLOCK
        table = jnp.where(i == 0, n_pairs, 0) if j == 0 else 0
        kvw = kvbuf[r0:r0 + 2 * BLOCK, :]
        k = kvw[:, :LANES]
        v = kvw[:, LANES:]
        kr = pltpu.roll(k, HEAD_DIM, 1)
        vr = pltpu.roll(v, HEAD_DIM, 1)
        k_lo, k_hi = (k, kr) if kh == 0 else (kr, k)
        v_lo, v_hi = (v, vr) if kh == 0 else (vr, v)
        kx = jnp.concatenate([jnp.where(keep_lo, k_lo, zero), jnp.where(keep_hi, k_hi, zero)], axis=0)
        vx = jnp.concatenate(
            [jnp.concatenate([jnp.where(keep_lo, v_lo, zero), ones_lo], axis=1),
             jnp.concatenate([jnp.where(keep_hi, v_hi, zero), ones_hi], axis=1)], axis=0)
        for pp in range(2):
            pair = kh * 2 + pp
            qp = q_ref[0, r0:r0 + BLOCK, pair * LANES:(pair + 1) * LANES]
            sc = lax.dot_general(qp, kx, (((1,), (1,)), ((), ())), preferred_element_type=F32)
            sc = sc + biasm[table + pair]
            es = []
            for half in range(2):
                st = sc[:, half * 2 * BLOCK:(half + 1) * 2 * BLOCK]
                m = jnp.max(st, axis=-1, keepdims=True)
                es.append(jnp.exp2(st - m).astype(BF16))
            pv = jnp.dot(jnp.concatenate(es, axis=1), vx, preferred_element_type=F32)
            attn_sc[r0:r0 + BLOCK, pair * LANES:(pair + 1) * LANES] = pv[:, :LANES] * (1.0 / pv[:, LANES:])

    def attn_finish(j):
        rows = slice(j * BLOCK, (j + 1) * BLOCK)
        a = attn_sc[rows, :]
        merged[rows, 0:ATTN_WIDTH] = (a * _rms_scale(a, ATTN_WIDTH) * ga_ref[...]).astype(BF16)

    def outproj(rows):
        y = jnp.dot(merged[rows, :], wo_ref[...], preferred_element_type=F32)
        x1_sc[rows, :] = x_ref[0, rows, :] + g1_ref[0] * y

    blocks_per_chunk = OUTPROJ_ROWS // BLOCK
    attn_units = []
    for j in range(nblk):
        attn_units.append(functools.partial(attn_group, j, 0))
        tail = [functools.partial(attn_group, j, 1), functools.partial(attn_finish, j)]
        if (j + 1) % blocks_per_chunk == 0:
            tail.append(functools.partial(outproj, slice((j + 1 - blocks_per_chunk) * BLOCK, (j + 1) * BLOCK)))
        attn_units.append(lambda fs=tail: [f() for f in fs])

    kvbuf[0:BLOCK, :] = kvp_ref[0]
    kvbuf[BLOCK:BLOCK + tq, :] = kv_ref[0]
    merged[:, ATTN_WIDTH:] = cv_ref[0]
    attn_units[0]()
    for rows in subs:
        ffn_norm(rows)
    rest = attn_units[1:]
    done = 0
    for n, unit in enumerate(ffn_units):
        unit()
        want = (n + 1) * len(rest) // len(ffn_units)
        for a in rest[done:want]:
            a()
        done = want


def _mixer_ffn(q, kv, convn, x, mod3, sinks, rel_bias, g_attn, w_out, g_norm2, w_gu, w_down, g_final,
               final_norm):
    b, s, _ = x.shape
    tq = TQ_ATTN
    nblk = tq // BLOCK
    tps = s // tq
    n_tiles = b * tps
    att = lambda t: (jnp.minimum(t, n_tiles - 1) // tps, jnp.minimum(t, n_tiles - 1) % tps)
    ffn = lambda t: (jnp.maximum(t - 1, 0) // tps, jnp.maximum(t - 1, 0) % tps)
    att_tok = lambda w: pl.BlockSpec((1, tq, w), lambda t: (*att(t), 0))
    att_mod = lambda k: pl.BlockSpec((1, 1, D_MODEL), lambda t: (att(t)[0], 0, k))
    ffn_mod = lambda k: pl.BlockSpec((1, 1, D_MODEL), lambda t: (ffn(t)[0], 0, k))
    const = lambda shape: pl.BlockSpec(shape, lambda t: (0,) * len(shape))
    resident = lambda shape: pl.BlockSpec(shape, lambda t: (0,) * len(shape), pipeline_mode=pl.Buffered(1))
    smem = pl.BlockSpec(memory_space=pltpu.SMEM)
    bucket = jnp.asarray(_bucket_table())
    return pl.pallas_call(
        functools.partial(_mixer_ffn_kernel, n_tiles=n_tiles, tiles_per_seq=tps, final_norm=final_norm),
        grid=(n_tiles + 1,),
        in_specs=[smem, smem, const((BLOCK, 2 * BLOCK)),
                  att_tok(ATTN_WIDTH), att_tok(2 * KV_WIDTH),
                  pl.BlockSpec((1, BLOCK, 2 * KV_WIDTH),
                               lambda t: (att(t)[0], jnp.maximum(att(t)[1] * nblk - 1, 0), 0)),
                  att_tok(CONV_WIDTH), att_tok(D_MODEL), att_mod(2),
                  ffn_mod(3), ffn_mod(4), ffn_mod(5),
                  const((1, ATTN_WIDTH)), resident((D_MODEL, D_MODEL)), const((1, D_MODEL)),
                  resident((D_MODEL, 2 * D_FF)), resident((D_FF, D_MODEL)), const((1, D_MODEL))],
        out_specs=pl.BlockSpec((1, tq, D_MODEL), lambda t: (*ffn(t), 0)),
        out_shape=jax.ShapeDtypeStruct((b, s, D_MODEL), F32),
        scratch_shapes=[pltpu.VMEM((N_Q_HEADS, BLOCK, 4 * BLOCK), F32),
                        pltpu.VMEM((tq + BLOCK, 2 * KV_WIDTH), BF16),
                        pltpu.VMEM((tq, ATTN_WIDTH), F32),
                        pltpu.VMEM((tq, D_MODEL), BF16),
                        pltpu.VMEM((tq, D_MODEL), F32),
                        pltpu.VMEM((tq, D_MODEL), F32),
                        pltpu.VMEM((tq, D_MODEL), BF16),
                        pltpu.VMEM((tq, D_FF), BF16)],
        compiler_params=pltpu.CompilerParams(
            dimension_semantics=("arbitrary",),
            vmem_limit_bytes=56 * 1024 * 1024),
        name="mixer_ffn",
    )(sinks, rel_bias, bucket, q, kv, kv, convn, x, mod3, mod3, mod3, mod3, g_attn, w_out, g_norm2,
      w_gu, w_down, g_final)


def kernel(x, c, rel_bias, w_ada, b_ada, g_norm1, w_in, sinks, conv_w, g_attn_out, g_conv_out,
           w_out, g_norm2, w_gu, w_down, g_final):
    depth = w_ada.shape[0]
    b = x.shape[0]
    for l in range(depth):
        mod3 = _adaln(c, w_ada[l], b_ada[l]).reshape(b, 1, N_MOD * D_MODEL)
        q, kv, convn, w_out_bf, w_gu_bf, w_down_bf = _inproj(
            x, mod3, g_norm1[l].reshape(1, D_MODEL), w_in[l].astype(BF16), conv_w[l],
            g_conv_out[l].reshape(1, CONV_WIDTH), w_out[l], w_gu[l], w_down[l])
        x = _mixer_ffn(q, kv, convn, x, mod3, sinks[l], rel_bias, g_attn_out[l].reshape(1, ATTN_WIDTH),
                       w_out_bf, g_norm2[l].reshape(1, D_MODEL), w_gu_bf, w_down_bf,
                       g_final.reshape(1, D_MODEL), final_norm=(l == depth - 1))
    return x
```

```python
import functools
import math

import jax
import jax.numpy as jnp
import numpy as np
from jax import lax
from jax.experimental import pallas as pl
from jax.experimental.pallas import tpu as pltpu

D_MODEL = 1024
HEAD_DIM = 64
N_Q_HEADS = 8
N_KV_HEADS = 2
ATTN_WIDTH = N_Q_HEADS * HEAD_DIM
KV_WIDTH = N_KV_HEADS * HEAD_DIM
WINDOW = 128
BLOCK = 128
N_BUCKETS = 32
MAX_DISTANCE = 128
CONV_WIDTH = D_MODEL - ATTN_WIDTH
CONV_K = 3
IN_PROJ_WIDTH = ATTN_WIDTH + 2 * KV_WIDTH + 3 * CONV_WIDTH
D_FF = -(-8 * D_MODEL // (3 * 256)) * 256
N_MOD = 6
EPS = 1e-6
NEG_INF = -1e30
LOG2E = math.log2(math.e)

LANES = 128
SUBLANES = 8
MXU_DIM = 256
VMEM_BYTES = 64 * 1024 * 1024

F32 = jnp.float32
BF16 = jnp.bfloat16

TM_PROJ = 1024
PROJ_SUB = 256
TQ_ATTN = 512
OUTPROJ_ROWS = 256
TM_FFN = 512
FFN_SUB = 256
ADA_TN = 1024
FF_CHUNK = 2 * MXU_DIM


def _bucket_table():
    qi = np.arange(BLOCK, dtype=np.int32)[:, None]
    sj = np.arange(2 * BLOCK, dtype=np.int32)[None, :]
    dist = qi + BLOCK - sj
    d0 = np.maximum(dist, 0)
    max_exact = N_BUCKETS // 2
    d = np.maximum(d0, 1).astype(np.float32)
    large = max_exact + (np.log(d / np.float32(max_exact)) / np.float32(math.log(MAX_DISTANCE / max_exact))
                         * np.float32(N_BUCKETS - max_exact)).astype(np.int32)
    large = np.minimum(large, N_BUCKETS - 1)
    bucket = np.where(d0 < max_exact, d0, large)
    in_window = (dist >= 0) & (dist < WINDOW)
    return np.where(in_window, bucket, -1).astype(np.int32)


def _rms_scale(v, width):
    return lax.rsqrt(jnp.sum(v * v, axis=-1, keepdims=True) * (1.0 / width) + EPS)


def _adaln_kernel(c_ref, w_ref, b_ref, o_ref):
    c = c_ref[...]
    cond = c * jax.nn.sigmoid(c)
    o_ref[...] = jnp.dot(cond.astype(BF16), w_ref[...].astype(BF16),
                         preferred_element_type=F32) + b_ref[...]


def _adaln(c, w_ada, b_ada):
    b = c.shape[0]
    n = w_ada.shape[1]
    return pl.pallas_call(
        _adaln_kernel,
        grid=(n // ADA_TN,),
        in_specs=[pl.BlockSpec((b, D_MODEL), lambda j: (0, 0)),
                  pl.BlockSpec((D_MODEL, ADA_TN), lambda j: (0, j)),
                  pl.BlockSpec((1, ADA_TN), lambda j: (0, j))],
        out_specs=pl.BlockSpec((b, ADA_TN), lambda j: (0, j)),
        out_shape=jax.ShapeDtypeStruct((b, n), F32),
        name="adaln_mod",
    )(c, w_ada, b_ada.reshape(1, n))


def _inproj_kernel(x_ref, sh_ref, sc_ref, gn_ref, w_ref, wo_ref, wgu_ref, wd_ref,
                   q_ref, kv_ref, gu_ref, wo_bf_ref, wgu_bf_ref, wd_bf_ref):
    tm = x_ref.shape[1]

    wo_bf_ref[...] = wo_ref[...].astype(BF16)
    wgu_bf_ref[...] = wgu_ref[...].astype(BF16)
    wd_bf_ref[...] = wd_ref[...].astype(BF16)

    gain = gn_ref[...] * (1.0 + sc_ref[0])
    shift = sh_ref[0]
    subs = [slice(r0, r0 + PROJ_SUB) for r0 in range(0, tm, PROJ_SUB)]
    hs = []
    for rows in subs:
        x = x_ref[0, rows, :]
        hs.append(((x * _rms_scale(x, D_MODEL)) * gain + shift).astype(BF16))
    c0 = ATTN_WIDTH + 2 * KV_WIDTH
    for rows, h in zip(subs, hs):
        proj = jnp.dot(h, w_ref[...], preferred_element_type=F32)
        q_ref[0, rows, :] = (proj[:, :ATTN_WIDTH] * (HEAD_DIM ** -0.5 * LOG2E)).astype(BF16)
        kv_ref[0, rows, :] = proj[:, ATTN_WIDTH:c0].astype(BF16)
        gu_ref[0, rows, :CONV_WIDTH] = proj[:, c0:c0 + CONV_WIDTH].astype(BF16)
        gu_ref[0, rows, CONV_WIDTH:] = (proj[:, c0 + CONV_WIDTH:c0 + 2 * CONV_WIDTH]
                                        * proj[:, c0 + 2 * CONV_WIDTH:]).astype(BF16)


def _inproj(x, mod3, g_norm1, w_in, w_out, w_gu, w_down):
    b, s, _ = x.shape
    tm = TM_PROJ
    tps = s // tm
    n_steps = b * tps
    tok = lambda w: pl.BlockSpec((1, tm, w), lambda bi, i: (bi, i, 0))
    modspec = lambda k: pl.BlockSpec((1, 1, D_MODEL), lambda bi, i: (bi, 0, k))
    const = lambda shape: pl.BlockSpec(shape, lambda bi, i: (0,) * len(shape))
    slabs = [w.reshape(n_steps, w.shape[0] // n_steps, w.shape[1]) for w in (w_out, w_gu, w_down)]
    slab = lambda w: pl.BlockSpec((1,) + w.shape[1:], lambda bi, i: (bi * tps + i, 0, 0))
    outs = pl.pallas_call(
        _inproj_kernel,
        grid=(b, tps),
        in_specs=[tok(D_MODEL), modspec(0), modspec(1), const((1, D_MODEL)),
                  const((D_MODEL, IN_PROJ_WIDTH)), *[slab(w) for w in slabs]],
        out_specs=[tok(ATTN_WIDTH), tok(2 * KV_WIDTH), tok(2 * CONV_WIDTH), *[slab(w) for w in slabs]],
        out_shape=[jax.ShapeDtypeStruct((b, s, ATTN_WIDTH), BF16),
                   jax.ShapeDtypeStruct((b, s, 2 * KV_WIDTH), BF16),
                   jax.ShapeDtypeStruct((b, s, 2 * CONV_WIDTH), BF16),
                   *[jax.ShapeDtypeStruct(w.shape, BF16) for w in slabs]],
        compiler_params=pltpu.CompilerParams(
            dimension_semantics=("parallel", "parallel"),
            vmem_limit_bytes=48 * 1024 * 1024),
        name="inproj",
    )(x, mod3, mod3, g_norm1, w_in, *slabs)
    q, kv, gate_u = outs[:3]
    return q, kv, gate_u, *[o.reshape(w.shape) for o, w in zip(outs[3:], (w_out, w_gu, w_down))]


def _mixer_ffn_kernel(sinks_ref, relb_ref, bucket_ref, q_ref, kv_ref, kvp_ref, gu_ref, x_ref, g1_ref,
                      sh2_ref, sc2_ref, g2_ref, ga_ref, cw_ref, gc_ref, wo_ref, gn2_ref, wgu_ref, wd_ref,
                      gf_ref, o_ref, biasm, kvbuf, ubuf, attn_sc, merged, x1_sc, x1_res, hbuf, act,
                      *, n_tiles, tiles_per_seq, final_norm):
    tq = q_ref.shape[1]
    nblk = tq // BLOCK
    t = pl.program_id(0)
    i = jnp.minimum(t, n_tiles - 1) % tiles_per_seq
    n_pairs = N_Q_HEADS // 2

    @pl.when(t == 0)
    def _():
        x1_sc[...] = jnp.zeros(x1_sc.shape, F32)
        ubuf[...] = jnp.zeros(ubuf.shape, F32)
        bk = bucket_ref[...]
        col = lax.broadcasted_iota(jnp.int32, (BLOCK, 2 * BLOCK), 1)
        for h in range(N_Q_HEADS):
            acc = jnp.full((BLOCK, 2 * BLOCK), NEG_INF, F32)
            for bb in range(N_BUCKETS):
                acc = jnp.where(bk == bb, relb_ref[bb, h] * LOG2E, acc)
            first = jnp.where(col < BLOCK, NEG_INF, acc)
            sink = sinks_ref[h] * LOG2E
            c0 = (h % 2) * 2 * BLOCK
            biasm[h // 2, :, c0:c0 + 2 * BLOCK] = jnp.where(col == 0, sink, acc)
            biasm[n_pairs + h // 2, :, c0:c0 + 2 * BLOCK] = jnp.where(col == 0, sink, first)

    gain = gn2_ref[...] * (1.0 + sc2_ref[0])
    shift = sh2_ref[0]
    subs = [slice(r0, r0 + FFN_SUB) for r0 in range(0, tq, FFN_SUB)]

    def ffn_norm(rows):
        x1 = x1_sc[rows, :]
        x1_res[rows, :] = x1
        hbuf[rows, :] = ((x1 * _rms_scale(x1, D_MODEL)) * gain + shift).astype(BF16)

    def ffn_chunk(rows, c0):
        c1 = min(c0 + FF_CHUNK, D_FF)
        h = hbuf[rows, :]
        gate = jnp.dot(h, wgu_ref[:, c0:c1], preferred_element_type=F32)
        up = jnp.dot(h, wgu_ref[:, D_FF + c0:D_FF + c1], preferred_element_type=F32)
        act[rows, c0:c1] = (gate * jax.nn.sigmoid(gate) * up).astype(BF16)

    def ffn_down(rows):
        y = jnp.dot(act[rows, :], wd_ref[...], preferred_element_type=F32)
        x2 = x1_res[rows, :] + g2_ref[0] * y
        o_ref[0, rows, :] = x2 * _rms_scale(x2, D_MODEL) * gf_ref[...] if final_norm else x2

    ffn_units = []
    for rows in subs:
        ffn_units += [functools.partial(ffn_chunk, rows, c0) for c0 in range(0, D_FF, FF_CHUNK)]
        ffn_units.append(functools.partial(ffn_down, rows))

    lane = lax.broadcasted_iota(jnp.int32, (2 * BLOCK, LANES), 1)
    real_key = lax.broadcasted_iota(jnp.int32, (2 * BLOCK, LANES), 0) > 0
    keep_lo = (lane < HEAD_DIM) & real_key
    keep_hi = (lane >= HEAD_DIM) & real_key
    zero = jnp.zeros((2 * BLOCK, LANES), BF16)
    ones_lo = jnp.where(lane < HEAD_DIM, 1.0, 0.0).astype(BF16)
    ones_hi = jnp.where(lane >= HEAD_DIM, 1.0, 0.0).astype(BF16)

    def attn_group(j, kh):
        r0 = j * BLOCK
        table = jnp.where(i == 0, n_pairs, 0) if j == 0 else 0
        kvw = kvbuf[r0:r0 + 2 * BLOCK, :]
        k = kvw[:, :LANES]
        v = kvw[:, LANES:]
        kr = pltpu.roll(k, HEAD_DIM, 1)
        vr = pltpu.roll(v, HEAD_DIM, 1)
        k_lo, k_hi = (k, kr) if kh == 0 else (kr, k)
        v_lo, v_hi = (v, vr) if kh == 0 else (vr, v)
        kx = jnp.concatenate([jnp.where(keep_lo, k_lo, zero), jnp.where(keep_hi, k_hi, zero)], axis=0)
        vx = jnp.concatenate(
            [jnp.concatenate([jnp.where(keep_lo, v_lo, zero), ones_lo], axis=1),
             jnp.concatenate([jnp.where(keep_hi, v_hi, zero), ones_hi], axis=1)], axis=0)
        for pp in range(2):
            pair = kh * 2 + pp
            qp = q_ref[0, r0:r0 + BLOCK, pair * LANES:(pair + 1) * LANES]
            sc = lax.dot_general(qp, kx, (((1,), (1,)), ((), ())), preferred_element_type=F32)
            sc = sc + biasm[table + pair]
            es = []
            for half in range(2):
                st = sc[:, half * 2 * BLOCK:(half + 1) * 2 * BLOCK]
                m = jnp.max(st, axis=-1, keepdims=True)
                es.append(jnp.exp2(st - m).astype(BF16))
            pv = jnp.dot(jnp.concatenate(es, axis=1), vx, preferred_element_type=F32)
            attn_sc[r0:r0 + BLOCK, pair * LANES:(pair + 1) * LANES] = pv[:, :LANES] * (1.0 / pv[:, LANES:])

    def conv_group(rows):
        n = rows.stop - rows.start
        gate_b = gu_ref[0, rows, :CONV_WIDTH].astype(F32)
        u = gu_ref[0, rows, CONV_WIDTH:].astype(F32)
        r0 = rows.start + SUBLANES
        ubuf[r0:r0 + n, :] = u
        ue = ubuf[r0 - SUBLANES:r0 + n, :]
        u1 = pltpu.roll(ue, 1, 0)[SUBLANES:, :]
        u2 = pltpu.roll(ue, 2, 0)[SUBLANES:, :]
        conv = gate_b * (cw_ref[0:1, :] * u2 + cw_ref[1:2, :] * u1 + cw_ref[2:3, :] * u)
        merged[rows, ATTN_WIDTH:] = (conv * _rms_scale(conv, CONV_WIDTH) * gc_ref[...]).astype(BF16)

    def attn_finish(j):
        rows = slice(j * BLOCK, (j + 1) * BLOCK)
        a = attn_sc[rows, :]
        merged[rows, 0:ATTN_WIDTH] = (a * _rms_scale(a, ATTN_WIDTH) * ga_ref[...]).astype(BF16)

    def outproj(rows):
        y = jnp.dot(merged[rows, :], wo_ref[...], preferred_element_type=F32)
        x1_sc[rows, :] = x_ref[0, rows, :] + g1_ref[0] * y

    blocks_per_chunk = OUTPROJ_ROWS // BLOCK
    attn_units = []
    half = BLOCK // 2
    for j in range(nblk + 1):
        head = []
        if j > 0 and j % blocks_per_chunk == 0:
            head.append(functools.partial(outproj, slice((j - blocks_per_chunk) * BLOCK, j * BLOCK)))
        if j < nblk:
            head += [functools.partial(conv_group, slice(j * BLOCK, j * BLOCK + half)),
                     functools.partial(attn_group, j, 0)]
        attn_units.append(lambda fs=head: [f() for f in fs])
        if j < nblk:
            tail = [functools.partial(conv_group, slice(j * BLOCK + half, (j + 1) * BLOCK)),
                    functools.partial(attn_group, j, 1), functools.partial(attn_finish, j)]
            attn_units.append(lambda fs=tail: [f() for f in fs])

    kvbuf[0:BLOCK, :] = kvp_ref[0]
    kvbuf[BLOCK:BLOCK + tq, :] = kv_ref[0]
    ubuf[0:SUBLANES, :] = jnp.where(i == 0, 0.0, ubuf[tq:tq + SUBLANES, :])
    attn_units[0]()
    for rows in subs:
        ffn_norm(rows)
    rest = attn_units[1:]
    done = 0
    for n, unit in enumerate(ffn_units):
        unit()
        want = min(len(rest), -(-(n + 1) * len(rest) // (len(ffn_units) - 1)))
        for a in rest[done:want]:
            a()
        done = want


def _mixer_ffn(q, kv, gate_u, x, mod3, sinks, rel_bias, g_attn, conv_w, g_conv, w_out, g_norm2, w_gu, w_down,
               g_final, final_norm):
    b, s, _ = x.shape
    tq = TQ_ATTN
    nblk = tq // BLOCK
    tps = s // tq
    n_tiles = b * tps
    att = lambda t: (jnp.minimum(t, n_tiles - 1) // tps, jnp.minimum(t, n_tiles - 1) % tps)
    ffn = lambda t: (jnp.maximum(t - 1, 0) // tps, jnp.maximum(t - 1, 0) % tps)
    att_tok = lambda w: pl.BlockSpec((1, tq, w), lambda t: (*att(t), 0))
    att_mod = lambda k: pl.BlockSpec((1, 1, D_MODEL), lambda t: (att(t)[0], 0, k))
    ffn_mod = lambda k: pl.BlockSpec((1, 1, D_MODEL), lambda t: (ffn(t)[0], 0, k))
    const = lambda shape: pl.BlockSpec(shape, lambda t: (0,) * len(shape))
    resident = lambda shape: pl.BlockSpec(shape, lambda t: (0,) * len(shape), pipeline_mode=pl.Buffered(1))
    smem = pl.BlockSpec(memory_space=pltpu.SMEM)
    bucket = jnp.asarray(_bucket_table())
    return pl.pallas_call(
        functools.partial(_mixer_ffn_kernel, n_tiles=n_tiles, tiles_per_seq=tps, final_norm=final_norm),
        grid=(n_tiles + 1,),
        in_specs=[smem, smem, const((BLOCK, 2 * BLOCK)),
                  att_tok(ATTN_WIDTH), att_tok(2 * KV_WIDTH),
                  pl.BlockSpec((1, BLOCK, 2 * KV_WIDTH),
                               lambda t: (att(t)[0], jnp.maximum(att(t)[1] * nblk - 1, 0), 0)),
                  att_tok(2 * CONV_WIDTH), att_tok(D_MODEL), att_mod(2),
                  ffn_mod(3), ffn_mod(4), ffn_mod(5),
                  const((1, ATTN_WIDTH)), const((CONV_K, CONV_WIDTH)), const((1, CONV_WIDTH)),
                  resident((D_MODEL, D_MODEL)), const((1, D_MODEL)),
                  resident((D_MODEL, 2 * D_FF)), resident((D_FF, D_MODEL)), const((1, D_MODEL))],
        out_specs=pl.BlockSpec((1, tq, D_MODEL), lambda t: (*ffn(t), 0)),
        out_shape=jax.ShapeDtypeStruct((b, s, D_MODEL), F32),
        scratch_shapes=[pltpu.VMEM((N_Q_HEADS, BLOCK, 4 * BLOCK), F32),
                        pltpu.VMEM((tq + BLOCK, 2 * KV_WIDTH), BF16),
                        pltpu.VMEM((tq + SUBLANES, CONV_WIDTH), F32),
                        pltpu.VMEM((tq, ATTN_WIDTH), F32),
                        pltpu.VMEM((tq, D_MODEL), BF16),
                        pltpu.VMEM((tq, D_MODEL), F32),
                        pltpu.VMEM((tq, D_MODEL), F32),
                        pltpu.VMEM((tq, D_MODEL), BF16),
                        pltpu.VMEM((tq, D_FF), BF16)],
        compiler_params=pltpu.CompilerParams(
            dimension_semantics=("arbitrary",),
            vmem_limit_bytes=56 * 1024 * 1024),
        name="mixer_ffn",
    )(sinks, rel_bias, bucket, q, kv, kv, gate_u, x, *[mod3] * 4, g_attn, conv_w, g_conv, w_out,
      g_norm2, w_gu, w_down, g_final)


def kernel(x, c, rel_bias, w_ada, b_ada, g_norm1, w_in, sinks, conv_w, g_attn_out, g_conv_out,
           w_out, g_norm2, w_gu, w_down, g_final):
    depth = w_ada.shape[0]
    b = x.shape[0]
    for l in range(depth):
        mod3 = _adaln(c, w_ada[l], b_ada[l]).reshape(b, 1, N_MOD * D_MODEL)
        q, kv, gate_u, w_out_bf, w_gu_bf, w_down_bf = _inproj(
            x, mod3, g_norm1[l].reshape(1, D_MODEL), w_in[l].astype(BF16), w_out[l], w_gu[l], w_down[l])
        x = _mixer_ffn(q, kv, gate_u, x, mod3, sinks[l], rel_bias, g_attn_out[l].reshape(1, ATTN_WIDTH),
                       conv_w[l], g_conv_out[l].reshape(1, CONV_WIDTH), w_out_bf,
                       g_norm2[l].reshape(1, D_MODEL), w_gu_bf, w_down_bf,
                       g_final.reshape(1, D_MODEL), final_norm=(l == depth - 1))
    return x
```

```python
import functools
import math

import jax
import jax.numpy as jnp
import numpy as np
from jax import lax
from jax.experimental import pallas as pl
from jax.experimental.pallas import tpu as pltpu

D_MODEL = 1024
HEAD_DIM = 64
N_Q_HEADS = 8
N_KV_HEADS = 2
ATTN_WIDTH = N_Q_HEADS * HEAD_DIM
KV_WIDTH = N_KV_HEADS * HEAD_DIM
WINDOW = 128
BLOCK = 128
N_BUCKETS = 32
MAX_DISTANCE = 128
CONV_WIDTH = D_MODEL - ATTN_WIDTH
CONV_K = 3
IN_PROJ_WIDTH = ATTN_WIDTH + 2 * KV_WIDTH + 3 * CONV_WIDTH
KV_COL = ATTN_WIDTH
GATE_COL = KV_COL + 2 * KV_WIDTH
U_COL = GATE_COL + CONV_WIDTH
PACK_WIDTH = U_COL + CONV_WIDTH
D_FF = -(-8 * D_MODEL // (3 * 256)) * 256
N_MOD = 6
EPS = 1e-6
NEG_INF = -1e30
LOG2E = math.log2(math.e)

LANES = 128
SUBLANES = 8
MXU_DIM = 256
VMEM_BYTES = 64 * 1024 * 1024

F32 = jnp.float32
BF16 = jnp.bfloat16

TM_PROJ = 1024
PROJ_SUB = 256
TQ_ATTN = 512
OUTPROJ_ROWS = 256
TM_FFN = 512
FFN_SUB = 256
ADA_TN = 1024
FF_CHUNK = 2 * MXU_DIM


def _bucket_table():
    qi = np.arange(BLOCK, dtype=np.int32)[:, None]
    sj = np.arange(2 * BLOCK, dtype=np.int32)[None, :]
    dist = qi + BLOCK - sj
    d0 = np.maximum(dist, 0)
    max_exact = N_BUCKETS // 2
    d = np.maximum(d0, 1).astype(np.float32)
    large = max_exact + (np.log(d / np.float32(max_exact)) / np.float32(math.log(MAX_DISTANCE / max_exact))
                         * np.float32(N_BUCKETS - max_exact)).astype(np.int32)
    large = np.minimum(large, N_BUCKETS - 1)
    bucket = np.where(d0 < max_exact, d0, large)
    in_window = (dist >= 0) & (dist < WINDOW)
    return np.where(in_window, bucket, -1).astype(np.int32)


def _rms_scale(v, width):
    return lax.rsqrt(jnp.sum(v * v, axis=-1, keepdims=True) * (1.0 / width) + EPS)


def _adaln_kernel(c_ref, w_ref, b_ref, o_ref):
    c = c_ref[...]
    cond = c * jax.nn.sigmoid(c)
    o_ref[...] = jnp.dot(cond.astype(BF16), w_ref[...].astype(BF16),
                         preferred_element_type=F32) + b_ref[...]


def _adaln(c, w_ada, b_ada):
    b = c.shape[0]
    n = w_ada.shape[1]
    return pl.pallas_call(
        _adaln_kernel,
        grid=(n // ADA_TN,),
        in_specs=[pl.BlockSpec((b, D_MODEL), lambda j: (0, 0)),
                  pl.BlockSpec((D_MODEL, ADA_TN), lambda j: (0, j)),
                  pl.BlockSpec((1, ADA_TN), lambda j: (0, j))],
        out_specs=pl.BlockSpec((b, ADA_TN), lambda j: (0, j)),
        out_shape=jax.ShapeDtypeStruct((b, n), F32),
        name="adaln_mod",
    )(c, w_ada, b_ada.reshape(1, n))


def _inproj_kernel(x_ref, sh_ref, sc_ref, gn_ref, w_ref, wo_ref, wgu_ref, wd_ref,
                   p_ref, wo_bf_ref, wgu_bf_ref, wd_bf_ref):
    tm = x_ref.shape[1]

    wo_bf_ref[...] = wo_ref[...].astype(BF16)
    wgu_bf_ref[...] = wgu_ref[...].astype(BF16)
    wd_bf_ref[...] = wd_ref[...].astype(BF16)

    gain = gn_ref[...] * (1.0 + sc_ref[0])
    shift = sh_ref[0]
    subs = [slice(r0, r0 + PROJ_SUB) for r0 in range(0, tm, PROJ_SUB)]
    hs = []
    for rows in subs:
        x = x_ref[0, rows, :]
        hs.append(((x * _rms_scale(x, D_MODEL)) * gain + shift).astype(BF16))
    for rows, h in zip(subs, hs):
        proj = jnp.dot(h, w_ref[...], preferred_element_type=F32)
        p_ref[0, rows, :KV_COL] = (proj[:, :KV_COL] * (HEAD_DIM ** -0.5 * LOG2E)).astype(BF16)
        p_ref[0, rows, KV_COL:U_COL] = proj[:, KV_COL:U_COL].astype(BF16)
        p_ref[0, rows, U_COL:] = (proj[:, U_COL:U_COL + CONV_WIDTH] * proj[:, U_COL + CONV_WIDTH:]).astype(BF16)


def _inproj(x, mod3, g_norm1, w_in, w_out, w_gu, w_down):
    b, s, _ = x.shape
    tm = TM_PROJ
    tps = s // tm
    n_steps = b * tps
    tok = lambda w: pl.BlockSpec((1, tm, w), lambda bi, i: (bi, i, 0))
    modspec = lambda k: pl.BlockSpec((1, 1, D_MODEL), lambda bi, i: (bi, 0, k))
    const = lambda shape: pl.BlockSpec(shape, lambda bi, i: (0,) * len(shape))
    slabs = [w.reshape(n_steps, w.shape[0] // n_steps, w.shape[1]) for w in (w_out, w_gu, w_down)]
    slab = lambda w: pl.BlockSpec((1,) + w.shape[1:], lambda bi, i: (bi * tps + i, 0, 0))
    outs = pl.pallas_call(
        _inproj_kernel,
        grid=(b, tps),
        in_specs=[tok(D_MODEL), modspec(0), modspec(1), const((1, D_MODEL)),
                  const((D_MODEL, IN_PROJ_WIDTH)), *[slab(w) for w in slabs]],
        out_specs=[tok(PACK_WIDTH), *[slab(w) for w in slabs]],
        out_shape=[jax.ShapeDtypeStruct((b, s, PACK_WIDTH), BF16),
                   *[jax.ShapeDtypeStruct(w.shape, BF16) for w in slabs]],
        compiler_params=pltpu.CompilerParams(
            dimension_semantics=("parallel", "parallel"),
            vmem_limit_bytes=48 * 1024 * 1024),
        name="inproj",
    )(x, mod3, mod3, g_norm1, w_in, *slabs)
    return outs[0], *[o.reshape(w.shape) for o, w in zip(outs[1:], (w_out, w_gu, w_down))]


def _mixer_ffn_kernel(sinks_ref, relb_ref, bucket_ref, p_ref, x_ref, g1_ref,
                      sh2_ref, sc2_ref, g2_ref, ga_ref, cw_ref, gc_ref, wo_ref, gn2_ref, wgu_ref, wd_ref,
                      gf_ref, o_ref, biasm, kcarry, ubuf, attn_sc, merged, x1_sc, x1_res, hbuf, act,
                      *, n_tiles, tiles_per_seq, final_norm):
    tq = p_ref.shape[1]
    nblk = tq // BLOCK
    t = pl.program_id(0)
    i = jnp.minimum(t, n_tiles - 1) % tiles_per_seq
    n_pairs = N_Q_HEADS // 2

    @pl.when(t == 0)
    def _():
        x1_sc[...] = jnp.zeros(x1_sc.shape, F32)
        ubuf[...] = jnp.zeros(ubuf.shape, F32)
        kcarry[...] = jnp.zeros(kcarry.shape, BF16)
        bk = bucket_ref[...]
        col = lax.broadcasted_iota(jnp.int32, (BLOCK, 2 * BLOCK), 1)
        for h in range(N_Q_HEADS):
            acc = jnp.full((BLOCK, 2 * BLOCK), NEG_INF, F32)
            for bb in range(N_BUCKETS):
                acc = jnp.where(bk == bb, relb_ref[bb, h] * LOG2E, acc)
            first = jnp.where(col < BLOCK, NEG_INF, acc)
            sink = sinks_ref[h] * LOG2E
            c0 = (h % 2) * 2 * BLOCK
            biasm[h // 2, :, c0:c0 + 2 * BLOCK] = jnp.where(col == 0, sink, acc)
            biasm[n_pairs + h // 2, :, c0:c0 + 2 * BLOCK] = jnp.where(col == 0, sink, first)

    gain = gn2_ref[...] * (1.0 + sc2_ref[0])
    shift = sh2_ref[0]
    subs = [slice(r0, r0 + FFN_SUB) for r0 in range(0, tq, FFN_SUB)]

    def ffn_norm(rows):
        x1 = x1_sc[rows, :]
        x1_res[rows, :] = x1
        hbuf[rows, :] = ((x1 * _rms_scale(x1, D_MODEL)) * gain + shift).astype(BF16)

    def ffn_chunk(rows, c0):
        c1 = min(c0 + FF_CHUNK, D_FF)
        h = hbuf[rows, :]
        gate = jnp.dot(h, wgu_ref[:, c0:c1], preferred_element_type=F32)
        up = jnp.dot(h, wgu_ref[:, D_FF + c0:D_FF + c1], preferred_element_type=F32)
        act[rows, c0:c1] = (gate * jax.nn.sigmoid(gate) * up).astype(BF16)

    def ffn_down(rows):
        y = jnp.dot(act[rows, :], wd_ref[...], preferred_element_type=F32)
        x2 = x1_res[rows, :] + g2_ref[0] * y
        o_ref[0, rows, :] = x2 * _rms_scale(x2, D_MODEL) * gf_ref[...] if final_norm else x2

    ffn_units = []
    for rows in subs:
        ffn_units += [functools.partial(ffn_chunk, rows, c0) for c0 in range(0, D_FF, FF_CHUNK)]
        ffn_units.append(functools.partial(ffn_down, rows))

    lane = lax.broadcasted_iota(jnp.int32, (2 * BLOCK, LANES), 1)
    real_key = lax.broadcasted_iota(jnp.int32, (2 * BLOCK, LANES), 0) > 0
    keep_lo = (lane < HEAD_DIM) & real_key
    keep_hi = (lane >= HEAD_DIM) & real_key
    zero = jnp.zeros((2 * BLOCK, LANES), BF16)
    ones_lo = jnp.where(lane < HEAD_DIM, 1.0, 0.0).astype(BF16)
    ones_hi = jnp.where(lane >= HEAD_DIM, 1.0, 0.0).astype(BF16)

    def attn_group(j, kh):
        r0 = j * BLOCK
        table = jnp.where(i == 0, n_pairs, 0) if j == 0 else 0
        if j == 0:
            kvw = jnp.concatenate([kcarry[...], p_ref[0, 0:BLOCK, KV_COL:GATE_COL]], axis=0)
        else:
            kvw = p_ref[0, r0 - BLOCK:r0 + BLOCK, KV_COL:GATE_COL]
        k = kvw[:, :LANES]
        v = kvw[:, LANES:]
        kr = pltpu.roll(k, HEAD_DIM, 1)
        vr = pltpu.roll(v, HEAD_DIM, 1)
        k_lo, k_hi = (k, kr) if kh == 0 else (kr, k)
        v_lo, v_hi = (v, vr) if kh == 0 else (vr, v)
        kx = jnp.concatenate([jnp.where(keep_lo, k_lo, zero), jnp.where(keep_hi, k_hi, zero)], axis=0)
        vx = jnp.concatenate(
            [jnp.concatenate([jnp.where(keep_lo, v_lo, zero), ones_lo], axis=1),
             jnp.concatenate([jnp.where(keep_hi, v_hi, zero), ones_hi], axis=1)], axis=0)
        for pp in range(2):
            pair = kh * 2 + pp
            qp = p_ref[0, r0:r0 + BLOCK, pair * LANES:(pair + 1) * LANES]
            sc = lax.dot_general(qp, kx, (((1,), (1,)), ((), ())), preferred_element_type=F32)
            sc = sc + biasm[table + pair]
            es = []
            for half in range(2):
                st = sc[:, half * 2 * BLOCK:(half + 1) * 2 * BLOCK]
                m = jnp.max(st, axis=-1, keepdims=True)
                es.append(jnp.exp2(st - m).astype(BF16))
            pv = jnp.dot(jnp.concatenate(es, axis=1), vx, preferred_element_type=F32)
            attn_sc[r0:r0 + BLOCK, pair * LANES:(pair + 1) * LANES] = pv[:, :LANES] * (1.0 / pv[:, LANES:])

    def conv_group(rows):
        n = rows.stop - rows.start
        gate_b = p_ref[0, rows, GATE_COL:U_COL].astype(F32)
        u = p_ref[0, rows, U_COL:].astype(F32)
        r0 = rows.start + SUBLANES
        ubuf[r0:r0 + n, :] = u
        ue = ubuf[r0 - SUBLANES:r0 + n, :]
        u1 = pltpu.roll(ue, 1, 0)[SUBLANES:, :]
        u2 = pltpu.roll(ue, 2, 0)[SUBLANES:, :]
        conv = gate_b * (cw_ref[0:1, :] * u2 + cw_ref[1:2, :] * u1 + cw_ref[2:3, :] * u)
        merged[rows, ATTN_WIDTH:] = (conv * _rms_scale(conv, CONV_WIDTH) * gc_ref[...]).astype(BF16)

    def attn_finish(j):
        rows = slice(j * BLOCK, (j + 1) * BLOCK)
        a = attn_sc[rows, :]
        merged[rows, 0:ATTN_WIDTH] = (a * _rms_scale(a, ATTN_WIDTH) * ga_ref[...]).astype(BF16)

    def outproj(rows):
        y = jnp.dot(merged[rows, :], wo_ref[...], preferred_element_type=F32)
        x1_sc[rows, :] = x_ref[0, rows, :] + g1_ref[0] * y

    blocks_per_chunk = OUTPROJ_ROWS // BLOCK
    attn_units = []
    half = BLOCK // 2
    for j in range(nblk + 1):
        head = []
        if j > 0 and j % blocks_per_chunk == 0:
            head.append(functools.partial(outproj, slice((j - blocks_per_chunk) * BLOCK, j * BLOCK)))
        if j < nblk:
            head += [functools.partial(conv_group, slice(j * BLOCK, j * BLOCK + half)),
                     functools.partial(attn_group, j, 0)]
        attn_units.append(lambda fs=head: [f() for f in fs])
        if j < nblk:
            tail = [functools.partial(conv_group, slice(j * BLOCK + half, (j + 1) * BLOCK)),
                    functools.partial(attn_group, j, 1), functools.partial(attn_finish, j)]
            attn_units.append(lambda fs=tail: [f() for f in fs])

    ubuf[0:SUBLANES, :] = jnp.where(i == 0, 0.0, ubuf[tq:tq + SUBLANES, :])
    attn_units[0]()
    for rows in subs:
        ffn_norm(rows)
    rest = attn_units[1:]
    done = 0
    for n, unit in enumerate(ffn_units):
        unit()
        want = min(len(rest), -(-(n + 1) * len(rest) // (len(ffn_units) - 1)))
        for a in rest[done:want]:
            a()
        done = want
    kcarry[...] = p_ref[0, tq - BLOCK:tq, KV_COL:GATE_COL]


def _mixer_ffn(packed, x, mod3, sinks, rel_bias, g_attn, conv_w, g_conv, w_out, g_norm2, w_gu, w_down,
               g_final, final_norm):
    b, s, _ = x.shape
    tq = TQ_ATTN
    nblk = tq // BLOCK
    tps = s // tq
    n_tiles = b * tps
    att = lambda t: (jnp.minimum(t, n_tiles - 1) // tps, jnp.minimum(t, n_tiles - 1) % tps)
    ffn = lambda t: (jnp.maximum(t - 1, 0) // tps, jnp.maximum(t - 1, 0) % tps)
    att_tok = lambda w: pl.BlockSpec((1, tq, w), lambda t: (*att(t), 0))
    att_mod = lambda k: pl.BlockSpec((1, 1, D_MODEL), lambda t: (att(t)[0], 0, k))
    ffn_mod = lambda k: pl.BlockSpec((1, 1, D_MODEL), lambda t: (ffn(t)[0], 0, k))
    const = lambda shape: pl.BlockSpec(shape, lambda t: (0,) * len(shape))
    resident = lambda shape: pl.BlockSpec(shape, lambda t: (0,) * len(shape), pipeline_mode=pl.Buffered(1))
    smem = pl.BlockSpec(memory_space=pltpu.SMEM)
    bucket = jnp.asarray(_bucket_table())
    return pl.pallas_call(
        functools.partial(_mixer_ffn_kernel, n_tiles=n_tiles, tiles_per_seq=tps, final_norm=final_norm),
        grid=(n_tiles + 1,),
        in_specs=[smem, smem, const((BLOCK, 2 * BLOCK)),
                  att_tok(PACK_WIDTH), att_tok(D_MODEL), att_mod(2),
                  ffn_mod(3), ffn_mod(4), ffn_mod(5),
                  const((1, ATTN_WIDTH)), const((CONV_K, CONV_WIDTH)), const((1, CONV_WIDTH)),
                  resident((D_MODEL, D_MODEL)), const((1, D_MODEL)),
                  resident((D_MODEL, 2 * D_FF)), resident((D_FF, D_MODEL)), const((1, D_MODEL))],
        out_specs=pl.BlockSpec((1, tq, D_MODEL), lambda t: (*ffn(t), 0)),
        out_shape=jax.ShapeDtypeStruct((b, s, D_MODEL), F32),
        scratch_shapes=[pltpu.VMEM((N_Q_HEADS, BLOCK, 4 * BLOCK), F32),
                        pltpu.VMEM((BLOCK, 2 * KV_WIDTH), BF16),
                        pltpu.VMEM((tq + SUBLANES, CONV_WIDTH), F32),
                        pltpu.VMEM((tq, ATTN_WIDTH), F32),
                        pltpu.VMEM((tq, D_MODEL), BF16),
                        pltpu.VMEM((tq, D_MODEL), F32),
                        pltpu.VMEM((tq, D_MODEL), F32),
                        pltpu.VMEM((tq, D_MODEL), BF16),
                        pltpu.VMEM((tq, D_FF), BF16)],
        compiler_params=pltpu.CompilerParams(
            dimension_semantics=("arbitrary",),
            vmem_limit_bytes=56 * 1024 * 1024),
        name="mixer_ffn",
    )(sinks, rel_bias, bucket, packed, x, *[mod3] * 4, g_attn, conv_w, g_conv, w_out,
      g_norm2, w_gu, w_down, g_final)


def kernel(x, c, rel_bias, w_ada, b_ada, g_norm1, w_in, sinks, conv_w, g_attn_out, g_conv_out,
           w_out, g_norm2, w_gu, w_down, g_final):
    depth = w_ada.shape[0]
    b = x.shape[0]
    for l in range(depth):
        mod3 = _adaln(c, w_ada[l], b_ada[l]).reshape(b, 1, N_MOD * D_MODEL)
        packed, w_out_bf, w_gu_bf, w_down_bf = _inproj(
            x, mod3, g_norm1[l].reshape(1, D_MODEL), w_in[l].astype(BF16), w_out[l], w_gu[l], w_down[l])
        x = _mixer_ffn(packed, x, mod3, sinks[l], rel_bias, g_attn_out[l].reshape(1, ATTN_WIDTH),
                       conv_w[l], g_conv_out[l].reshape(1, CONV_WIDTH), w_out_bf,
                       g_norm2[l].reshape(1, D_MODEL), w_gu_bf, w_down_bf,
                       g_final.reshape(1, D_MODEL), final_norm=(l == depth - 1))
    return x
```

```python
import functools
import math

import jax
import jax.numpy as jnp
import numpy as np
from jax import lax
from jax.experimental import pallas as pl
from jax.experimental.pallas import tpu as pltpu

D_MODEL = 1024
HEAD_DIM = 64
N_Q_HEADS = 8
N_KV_HEADS = 2
ATTN_WIDTH = N_Q_HEADS * HEAD_DIM
KV_WIDTH = N_KV_HEADS * HEAD_DIM
WINDOW = 128
BLOCK = 128
N_BUCKETS = 32
MAX_DISTANCE = 128
CONV_WIDTH = D_MODEL - ATTN_WIDTH
CONV_K = 3
IN_PROJ_WIDTH = ATTN_WIDTH + 2 * KV_WIDTH + 3 * CONV_WIDTH
KV_COL = ATTN_WIDTH
GATE_COL = KV_COL + 2 * KV_WIDTH
U_COL = GATE_COL + CONV_WIDTH
PACK_WIDTH = U_COL + CONV_WIDTH
D_FF = -(-8 * D_MODEL // (3 * 256)) * 256
N_MOD = 6
EPS = 1e-6
NEG_INF = -1e30
LOG2E = math.log2(math.e)

LANES = 128
SUBLANES = 8
MXU_DIM = 256
VMEM_BYTES = 64 * 1024 * 1024

F32 = jnp.float32
BF16 = jnp.bfloat16

TM_PROJ = 1024
PROJ_SUB = 256
TQ_ATTN = 512
OUTPROJ_ROWS = 256
TM_FFN = 512
FFN_SUB = 256
ADA_TN = 1024
FF_CHUNK = 2 * MXU_DIM


def _bucket_table():
    qi = np.arange(BLOCK, dtype=np.int32)[:, None]
    sj = np.arange(2 * BLOCK, dtype=np.int32)[None, :]
    dist = qi + BLOCK - sj
    d0 = np.maximum(dist, 0)
    max_exact = N_BUCKETS // 2
    d = np.maximum(d0, 1).astype(np.float32)
    large = max_exact + (np.log(d / np.float32(max_exact)) / np.float32(math.log(MAX_DISTANCE / max_exact))
                         * np.float32(N_BUCKETS - max_exact)).astype(np.int32)
    large = np.minimum(large, N_BUCKETS - 1)
    bucket = np.where(d0 < max_exact, d0, large)
    in_window = (dist >= 0) & (dist < WINDOW)
    return np.where(in_window, bucket, -1).astype(np.int32)


def _rms_scale(v, width):
    return lax.rsqrt(jnp.sum(v * v, axis=-1, keepdims=True) * (1.0 / width) + EPS)


def _adaln_kernel(c_ref, w_ref, b_ref, o_ref):
    c = c_ref[...]
    cond = c * jax.nn.sigmoid(c)
    o_ref[...] = jnp.dot(cond.astype(BF16), w_ref[...].astype(BF16),
                         preferred_element_type=F32) + b_ref[...]


def _adaln(c, w_ada, b_ada):
    b = c.shape[0]
    n = w_ada.shape[1]
    return pl.pallas_call(
        _adaln_kernel,
        grid=(n // ADA_TN,),
        in_specs=[pl.BlockSpec((b, D_MODEL), lambda j: (0, 0)),
                  pl.BlockSpec((D_MODEL, ADA_TN), lambda j: (0, j)),
                  pl.BlockSpec((1, ADA_TN), lambda j: (0, j))],
        out_specs=pl.BlockSpec((b, ADA_TN), lambda j: (0, j)),
        out_shape=jax.ShapeDtypeStruct((b, n), F32),
        name="adaln_mod",
    )(c, w_ada, b_ada.reshape(1, n))


def _inproj_kernel(x_ref, sh_ref, sc_ref, gn_ref, w_ref, wo_ref, wgu_ref, wd_ref,
                   p_ref, wo_bf_ref, wgu_bf_ref, wd_bf_ref):
    tm = x_ref.shape[1]

    wo_bf_ref[...] = wo_ref[...].astype(BF16)
    wgu_bf_ref[...] = wgu_ref[...].astype(BF16)
    wd_bf_ref[...] = wd_ref[...].astype(BF16)

    gain = gn_ref[...] * (1.0 + sc_ref[0])
    shift = sh_ref[0]
    subs = [slice(r0, r0 + PROJ_SUB) for r0 in range(0, tm, PROJ_SUB)]
    hs = []
    for rows in subs:
        x = x_ref[0, rows, :]
        hs.append(((x * _rms_scale(x, D_MODEL)) * gain + shift).astype(BF16))
    for rows, h in zip(subs, hs):
        proj = jnp.dot(h, w_ref[...], preferred_element_type=F32)
        p_ref[0, rows, :KV_COL] = (proj[:, :KV_COL] * (HEAD_DIM ** -0.5 * LOG2E)).astype(BF16)
        p_ref[0, rows, KV_COL:U_COL] = proj[:, KV_COL:U_COL].astype(BF16)
        p_ref[0, rows, U_COL:] = (proj[:, U_COL:U_COL + CONV_WIDTH] * proj[:, U_COL + CONV_WIDTH:]).astype(BF16)


def _inproj(x, mod3, g_norm1, w_in, w_out, w_gu, w_down):
    b, s, _ = x.shape
    tm = TM_PROJ
    tps = s // tm
    n_steps = b * tps
    tok = lambda w: pl.BlockSpec((1, tm, w), lambda bi, i: (bi, i, 0))
    modspec = lambda k: pl.BlockSpec((1, 1, D_MODEL), lambda bi, i: (bi, 0, k))
    const = lambda shape: pl.BlockSpec(shape, lambda bi, i: (0,) * len(shape))
    slabs = [w.reshape(n_steps, w.shape[0] // n_steps, w.shape[1]) for w in (w_out, w_gu, w_down)]
    slab = lambda w: pl.BlockSpec((1,) + w.shape[1:], lambda bi, i: (bi * tps + i, 0, 0))
    outs = pl.pallas_call(
        _inproj_kernel,
        grid=(b, tps),
        in_specs=[tok(D_MODEL), modspec(0), modspec(1), const((1, D_MODEL)),
                  const((D_MODEL, IN_PROJ_WIDTH)), *[slab(w) for w in slabs]],
        out_specs=[tok(PACK_WIDTH), *[slab(w) for w in slabs]],
        out_shape=[jax.ShapeDtypeStruct((b, s, PACK_WIDTH), BF16),
                   *[jax.ShapeDtypeStruct(w.shape, BF16) for w in slabs]],
        compiler_params=pltpu.CompilerParams(
            dimension_semantics=("parallel", "parallel"),
            vmem_limit_bytes=48 * 1024 * 1024),
        name="inproj",
    )(x, mod3, mod3, g_norm1, w_in, *slabs)
    return outs[0], *[o.reshape(w.shape) for o, w in zip(outs[1:], (w_out, w_gu, w_down))]


def _mixer_ffn_kernel(sinks_ref, relb_ref, bucket_ref, p_ref, x_ref, g1_ref, xlast_ref, g1last_ref,
                      sh2_ref, sc2_ref, g2_ref, ga_ref, cw_ref, gc_ref, wo_ref, gn2_ref, wgu_ref, wd_ref,
                      gf_ref, o_ref, biasm, kcarry, ubuf, e_sc, attn_sc, merged, x1_sc, x1_res, hbuf, act,
                      *, n_tiles, tiles_per_seq, final_norm):
    tq = p_ref.shape[1]
    nblk = tq // BLOCK
    t = pl.program_id(0)
    i = jnp.minimum(t, n_tiles - 1) % tiles_per_seq
    n_pairs = N_Q_HEADS // 2

    @pl.when(t == 0)
    def _():
        x1_sc[...] = jnp.zeros(x1_sc.shape, F32)
        merged[...] = jnp.zeros(merged.shape, BF16)
        ubuf[...] = jnp.zeros(ubuf.shape, F32)
        kcarry[...] = jnp.zeros(kcarry.shape, BF16)
        bk = bucket_ref[...]
        col = lax.broadcasted_iota(jnp.int32, (BLOCK, 2 * BLOCK), 1)
        for h in range(N_Q_HEADS):
            acc = jnp.full((BLOCK, 2 * BLOCK), NEG_INF, F32)
            for bb in range(N_BUCKETS):
                acc = jnp.where(bk == bb, relb_ref[bb, h] * LOG2E, acc)
            first = jnp.where(col < BLOCK, NEG_INF, acc)
            sink = sinks_ref[h] * LOG2E
            c0 = (h % 2) * 2 * BLOCK
            biasm[h // 2, :, c0:c0 + 2 * BLOCK] = jnp.where(col == 0, sink, acc)
            biasm[n_pairs + h // 2, :, c0:c0 + 2 * BLOCK] = jnp.where(col == 0, sink, first)

    gain = gn2_ref[...] * (1.0 + sc2_ref[0])
    shift = sh2_ref[0]
    subs = [slice(r0, r0 + FFN_SUB) for r0 in range(0, tq, FFN_SUB)]

    def ffn_norm(rows):
        x1 = x1_sc[rows, :]
        x1_res[rows, :] = x1
        hbuf[rows, :] = ((x1 * _rms_scale(x1, D_MODEL)) * gain + shift).astype(BF16)

    def ffn_chunk(rows, c0):
        c1 = min(c0 + FF_CHUNK, D_FF)
        h = hbuf[rows, :]
        gate = jnp.dot(h, wgu_ref[:, c0:c1], preferred_element_type=F32)
        up = jnp.dot(h, wgu_ref[:, D_FF + c0:D_FF + c1], preferred_element_type=F32)
        act[rows, c0:c1] = (gate * jax.nn.sigmoid(gate) * up).astype(BF16)

    def ffn_down(rows):
        y = jnp.dot(act[rows, :], wd_ref[...], preferred_element_type=F32)
        x2 = x1_res[rows, :] + g2_ref[0] * y
        o_ref[0, rows, :] = x2 * _rms_scale(x2, D_MODEL) * gf_ref[...] if final_norm else x2

    ffn_units = []
    for rows in subs:
        ffn_units += [functools.partial(ffn_chunk, rows, c0) for c0 in range(0, D_FF, FF_CHUNK)]
        ffn_units.append(functools.partial(ffn_down, rows))

    lane = lax.broadcasted_iota(jnp.int32, (2 * BLOCK, LANES), 1)
    real_key = lax.broadcasted_iota(jnp.int32, (2 * BLOCK, LANES), 0) > 0
    keep_lo = (lane < HEAD_DIM) & real_key
    keep_hi = (lane >= HEAD_DIM) & real_key
    zero = jnp.zeros((2 * BLOCK, LANES), BF16)
    ones_lo = jnp.where(lane < HEAD_DIM, 1.0, 0.0).astype(BF16)
    ones_hi = jnp.where(lane >= HEAD_DIM, 1.0, 0.0).astype(BF16)

    def key_value_window(j, col0):
        cols = slice(col0, col0 + LANES)
        if j == 0:
            return jnp.concatenate([kcarry[:, col0 - KV_COL:col0 - KV_COL + LANES], p_ref[0, 0:BLOCK, cols]],
                                   axis=0)
        return p_ref[0, (j - 1) * BLOCK:(j + 1) * BLOCK, cols]

    def pair_operand(w, kh):
        wr = pltpu.roll(w, HEAD_DIM, 1)
        w_lo, w_hi = (w, wr) if kh == 0 else (wr, w)
        return jnp.where(keep_lo, w_lo, zero), jnp.where(keep_hi, w_hi, zero)

    def attn_scores(j, kh):
        r0 = j * BLOCK
        table = jnp.where(i == 0, n_pairs, 0) if j == 0 else 0
        kx = jnp.concatenate(pair_operand(key_value_window(j, KV_COL), kh), axis=0)
        for pp in range(2):
            pair = kh * 2 + pp
            qp = p_ref[0, r0:r0 + BLOCK, pair * LANES:(pair + 1) * LANES]
            sc = lax.dot_general(qp, kx, (((1,), (1,)), ((), ())), preferred_element_type=F32)
            sc = sc + biasm[table + pair]
            for half in range(2):
                cols = slice(half * 2 * BLOCK, (half + 1) * 2 * BLOCK)
                st = sc[:, cols]
                e_sc[pp, :, cols] = jnp.exp2(st - jnp.max(st, axis=-1, keepdims=True)).astype(BF16)

    def attn_values(j, kh):
        r0 = j * BLOCK
        v_lo, v_hi = pair_operand(key_value_window(j, KV_COL + LANES), kh)
        vx = jnp.concatenate([jnp.concatenate([v_lo, ones_lo], axis=1),
                              jnp.concatenate([v_hi, ones_hi], axis=1)], axis=0)
        for pp in range(2):
            pair = kh * 2 + pp
            pv = jnp.dot(e_sc[pp], vx, preferred_element_type=F32)
            attn_sc[r0:r0 + BLOCK, pair * LANES:(pair + 1) * LANES] = pv[:, :LANES] * (1.0 / pv[:, LANES:])

    def conv_group(rows):
        n = rows.stop - rows.start
        gate_b = p_ref[0, rows, GATE_COL:U_COL].astype(F32)
        u = p_ref[0, rows, U_COL:].astype(F32)
        r0 = rows.start + SUBLANES
        ubuf[r0:r0 + n, :] = u
        ue = ubuf[r0 - SUBLANES:r0 + n, :]
        u1 = pltpu.roll(ue, 1, 0)[SUBLANES:, :]
        u2 = pltpu.roll(ue, 2, 0)[SUBLANES:, :]
        conv = gate_b * (cw_ref[0:1, :] * u2 + cw_ref[1:2, :] * u1 + cw_ref[2:3, :] * u)
        merged[rows, ATTN_WIDTH:] = (conv * _rms_scale(conv, CONV_WIDTH) * gc_ref[...]).astype(BF16)

    def attn_finish(j):
        rows = slice(j * BLOCK, (j + 1) * BLOCK)
        a = attn_sc[rows, :]
        merged[rows, 0:ATTN_WIDTH] = (a * _rms_scale(a, ATTN_WIDTH) * ga_ref[...]).astype(BF16)

    def outproj(rows):
        y = jnp.dot(merged[rows, :], wo_ref[...], preferred_element_type=F32)
        x1_sc[rows, :] = x_ref[0, rows, :] + g1_ref[0] * y

    last_rows = slice(tq - OUTPROJ_ROWS, tq)

    def outproj_last_of_previous_tile():
        y = jnp.dot(merged[last_rows, :], wo_ref[...], preferred_element_type=F32)
        x1_sc[last_rows, :] = xlast_ref[0] + g1last_ref[0] * y

    blocks_per_chunk = OUTPROJ_ROWS // BLOCK
    conv_rows = BLOCK // N_KV_HEADS
    phases = [(j, kh) for j in range(nblk) for kh in range(N_KV_HEADS)]
    attn_units = []
    due = None
    for n in range(len(phases) + 1):
        fs = []
        if due is not None:
            fs.append(functools.partial(outproj, due))
            due = None
        if n > 0:
            j, kh = phases[n - 1]
            fs.append(functools.partial(attn_values, j, kh))
            if kh == N_KV_HEADS - 1:
                fs.append(functools.partial(attn_finish, j))
                if (j + 1) % blocks_per_chunk == 0 and j + 1 < nblk:
                    due = slice((j + 1 - blocks_per_chunk) * BLOCK, (j + 1) * BLOCK)
        if n < len(phases):
            j, kh = phases[n]
            c0 = j * BLOCK + kh * conv_rows
            fs += [functools.partial(conv_group, slice(c0, c0 + conv_rows)),
                   functools.partial(attn_scores, j, kh)]
        attn_units.append(lambda fs=fs: [f() for f in fs])

    ubuf[0:SUBLANES, :] = jnp.where(i == 0, 0.0, ubuf[tq:tq + SUBLANES, :])
    outproj_last_of_previous_tile()
    attn_units[0]()
    for rows in subs:
        ffn_norm(rows)
    rest = attn_units[1:]
    done = 0
    for n, unit in enumerate(ffn_units):
        unit()
        want = min(len(rest), -(-(n + 1) * len(rest) // (len(ffn_units) - 1)))
        for a in rest[done:want]:
            a()
        done = want
    kcarry[...] = p_ref[0, tq - BLOCK:tq, KV_COL:GATE_COL]


def _mixer_ffn(packed, x, mod3, sinks, rel_bias, g_attn, conv_w, g_conv, w_out, g_norm2, w_gu, w_down,
               g_final, final_norm):
    b, s, _ = x.shape
    tq = TQ_ATTN
    nblk = tq // BLOCK
    tps = s // tq
    n_tiles = b * tps
    att = lambda t: (jnp.minimum(t, n_tiles - 1) // tps, jnp.minimum(t, n_tiles - 1) % tps)
    ffn = lambda t: (jnp.maximum(t - 1, 0) // tps, jnp.maximum(t - 1, 0) % tps)
    att_tok = lambda w: pl.BlockSpec((1, tq, w), lambda t: (*att(t), 0))
    att_mod = lambda k: pl.BlockSpec((1, 1, D_MODEL), lambda t: (att(t)[0], 0, k))
    ffn_mod = lambda k: pl.BlockSpec((1, 1, D_MODEL), lambda t: (ffn(t)[0], 0, k))
    const = lambda shape: pl.BlockSpec(shape, lambda t: (0,) * len(shape))
    resident = lambda shape: pl.BlockSpec(shape, lambda t: (0,) * len(shape), pipeline_mode=pl.Buffered(1))
    smem = pl.BlockSpec(memory_space=pltpu.SMEM)
    bucket = jnp.asarray(_bucket_table())
    return pl.pallas_call(
        functools.partial(_mixer_ffn_kernel, n_tiles=n_tiles, tiles_per_seq=tps, final_norm=final_norm),
        grid=(n_tiles + 1,),
        in_specs=[smem, smem, const((BLOCK, 2 * BLOCK)),
                  att_tok(PACK_WIDTH), att_tok(D_MODEL), att_mod(2),
                  pl.BlockSpec((1, OUTPROJ_ROWS, D_MODEL),
                               lambda t: (ffn(t)[0], (ffn(t)[1] + 1) * (tq // OUTPROJ_ROWS) - 1, 0)),
                  ffn_mod(2),
                  ffn_mod(3), ffn_mod(4), ffn_mod(5),
                  const((1, ATTN_WIDTH)), const((CONV_K, CONV_WIDTH)), const((1, CONV_WIDTH)),
                  resident((D_MODEL, D_MODEL)), const((1, D_MODEL)),
                  resident((D_MODEL, 2 * D_FF)), resident((D_FF, D_MODEL)), const((1, D_MODEL))],
        out_specs=pl.BlockSpec((1, tq, D_MODEL), lambda t: (*ffn(t), 0)),
        out_shape=jax.ShapeDtypeStruct((b, s, D_MODEL), F32),
        scratch_shapes=[pltpu.VMEM((N_Q_HEADS, BLOCK, 4 * BLOCK), F32),
                        pltpu.VMEM((BLOCK, 2 * KV_WIDTH), BF16),
                        pltpu.VMEM((tq + SUBLANES, CONV_WIDTH), F32),
                        pltpu.VMEM((2, BLOCK, 4 * BLOCK), BF16),
                        pltpu.VMEM((tq, ATTN_WIDTH), F32),
                        pltpu.VMEM((tq, D_MODEL), BF16),
                        pltpu.VMEM((tq, D_MODEL), F32),
                        pltpu.VMEM((tq, D_MODEL), F32),
                        pltpu.VMEM((tq, D_MODEL), BF16),
                        pltpu.VMEM((tq, D_FF), BF16)],
        compiler_params=pltpu.CompilerParams(
            dimension_semantics=("arbitrary",),
            vmem_limit_bytes=56 * 1024 * 1024),
        name="mixer_ffn",
    )(sinks, rel_bias, bucket, packed, x, mod3, x, *[mod3] * 4, g_attn, conv_w, g_conv, w_out,
      g_norm2, w_gu, w_down, g_final)


def kernel(x, c, rel_bias, w_ada, b_ada, g_norm1, w_in, sinks, conv_w, g_attn_out, g_conv_out,
           w_out, g_norm2, w_gu, w_down, g_final):
    depth = w_ada.shape[0]
    b = x.shape[0]
    for l in range(depth):
        mod3 = _adaln(c, w_ada[l], b_ada[l]).reshape(b, 1, N_MOD * D_MODEL)
        packed, w_out_bf, w_gu_bf, w_down_bf = _inproj(
            x, mod3, g_norm1[l].reshape(1, D_MODEL), w_in[l].astype(BF16), w_out[l], w_gu[l], w_down[l])
        x = _mixer_ffn(packed, x, mod3, sinks[l], rel_bias, g_attn_out[l].reshape(1, ATTN_WIDTH),
                       conv_w[l], g_conv_out[l].reshape(1, CONV_WIDTH), w_out_bf,
                       g_norm2[l].reshape(1, D_MODEL), w_gu_bf, w_down_bf,
                       g_final.reshape(1, D_MODEL), final_norm=(l == depth - 1))
    return x
```

```python
import functools
import math

import jax
import jax.numpy as jnp
import numpy as np
from jax import lax
from jax.experimental import pallas as pl
from jax.experimental.pallas import tpu as pltpu

D_MODEL = 1024
HEAD_DIM = 64
N_Q_HEADS = 8
N_KV_HEADS = 2
ATTN_WIDTH = N_Q_HEADS * HEAD_DIM
KV_WIDTH = N_KV_HEADS * HEAD_DIM
WINDOW = 128
BLOCK = 128
N_BUCKETS = 32
MAX_DISTANCE = 128
CONV_WIDTH = D_MODEL - ATTN_WIDTH
CONV_K = 3
IN_PROJ_WIDTH = ATTN_WIDTH + 2 * KV_WIDTH + 3 * CONV_WIDTH
KV_COL = ATTN_WIDTH
GATE_COL = KV_COL + 2 * KV_WIDTH
U_COL = GATE_COL + CONV_WIDTH
PACK_WIDTH = U_COL + CONV_WIDTH
D_FF = -(-8 * D_MODEL // (3 * 256)) * 256
N_MOD = 6
EPS = 1e-6
NEG_INF = -1e30
LOG2E = math.log2(math.e)

LANES = 128
SUBLANES = 8
MXU_DIM = 256
VMEM_BYTES = 64 * 1024 * 1024

F32 = jnp.float32
BF16 = jnp.bfloat16

TM_PROJ = 1024
PROJ_SUB = 256
TQ_ATTN = 512
OUTPROJ_ROWS = 256
TM_FFN = 512
FFN_SUB = 256
ADA_TN = 1024
FF_CHUNK = 2 * MXU_DIM


def _bucket_table():
    qi = np.arange(BLOCK, dtype=np.int32)[:, None]
    sj = np.arange(2 * BLOCK, dtype=np.int32)[None, :]
    dist = qi + BLOCK - sj
    d0 = np.maximum(dist, 0)
    max_exact = N_BUCKETS // 2
    d = np.maximum(d0, 1).astype(np.float32)
    large = max_exact + (np.log(d / np.float32(max_exact)) / np.float32(math.log(MAX_DISTANCE / max_exact))
                         * np.float32(N_BUCKETS - max_exact)).astype(np.int32)
    large = np.minimum(large, N_BUCKETS - 1)
    bucket = np.where(d0 < max_exact, d0, large)
    in_window = (dist >= 0) & (dist < WINDOW)
    return np.where(in_window, bucket, -1).astype(np.int32)


def _rms_scale(v, width):
    return lax.rsqrt(jnp.sum(v * v, axis=-1, keepdims=True) * (1.0 / width) + EPS)


def _adaln_kernel(c_ref, w_ref, b_ref, o_ref):
    c = c_ref[...]
    cond = c * jax.nn.sigmoid(c)
    o_ref[...] = jnp.dot(cond.astype(BF16), w_ref[...].astype(BF16),
                         preferred_element_type=F32) + b_ref[...]


def _adaln(c, w_ada, b_ada):
    b = c.shape[0]
    n = w_ada.shape[1]
    return pl.pallas_call(
        _adaln_kernel,
        grid=(n // ADA_TN,),
        in_specs=[pl.BlockSpec((b, D_MODEL), lambda j: (0, 0)),
                  pl.BlockSpec((D_MODEL, ADA_TN), lambda j: (0, j)),
                  pl.BlockSpec((1, ADA_TN), lambda j: (0, j))],
        out_specs=pl.BlockSpec((b, ADA_TN), lambda j: (0, j)),
        out_shape=jax.ShapeDtypeStruct((b, n), F32),
        name="adaln_mod",
    )(c, w_ada, b_ada.reshape(1, n))


def _inproj_kernel(x_ref, sh_ref, sc_ref, gn_ref, w_ref, wo_ref, wgu_ref, wd_ref,
                   p_ref, wo_bf_ref, wgu_bf_ref, wd_bf_ref):
    tm = x_ref.shape[1]

    wo_bf_ref[...] = wo_ref[...].astype(BF16)
    wgu_bf_ref[...] = wgu_ref[...].astype(BF16)
    wd_bf_ref[...] = wd_ref[...].astype(BF16)

    gain = gn_ref[...] * (1.0 + sc_ref[0])
    shift = sh_ref[0]
    subs = [slice(r0, r0 + PROJ_SUB) for r0 in range(0, tm, PROJ_SUB)]
    hs = []
    for rows in subs:
        x = x_ref[0, rows, :]
        hs.append(((x * _rms_scale(x, D_MODEL)) * gain + shift).astype(BF16))
    for rows, h in zip(subs, hs):
        proj = jnp.dot(h, w_ref[...], preferred_element_type=F32)
        p_ref[0, rows, :KV_COL] = (proj[:, :KV_COL] * (HEAD_DIM ** -0.5 * LOG2E)).astype(BF16)
        p_ref[0, rows, KV_COL:U_COL] = proj[:, KV_COL:U_COL].astype(BF16)
        p_ref[0, rows, U_COL:] = (proj[:, U_COL:U_COL + CONV_WIDTH] * proj[:, U_COL + CONV_WIDTH:]).astype(BF16)


def _inproj(x, mod3, g_norm1, w_in, w_out, w_gu, w_down):
    b, s, _ = x.shape
    tm = TM_PROJ
    tps = s // tm
    n_steps = b * tps
    tok = lambda w: pl.BlockSpec((1, tm, w), lambda bi, i: (bi, i, 0))
    modspec = lambda k: pl.BlockSpec((1, 1, D_MODEL), lambda bi, i: (bi, 0, k))
    const = lambda shape: pl.BlockSpec(shape, lambda bi, i: (0,) * len(shape))
    slabs = [w.reshape(n_steps, w.shape[0] // n_steps, w.shape[1]) for w in (w_out, w_gu, w_down)]
    slab = lambda w: pl.BlockSpec((1,) + w.shape[1:], lambda bi, i: (bi * tps + i, 0, 0))
    outs = pl.pallas_call(
        _inproj_kernel,
        grid=(b, tps),
        in_specs=[tok(D_MODEL), modspec(0), modspec(1), const((1, D_MODEL)),
                  const((D_MODEL, IN_PROJ_WIDTH)), *[slab(w) for w in slabs]],
        out_specs=[tok(PACK_WIDTH), *[slab(w) for w in slabs]],
        out_shape=[jax.ShapeDtypeStruct((b, s, PACK_WIDTH), BF16),
                   *[jax.ShapeDtypeStruct(w.shape, BF16) for w in slabs]],
        compiler_params=pltpu.CompilerParams(
            dimension_semantics=("parallel", "parallel"),
            vmem_limit_bytes=48 * 1024 * 1024),
        name="inproj",
    )(x, mod3, mod3, g_norm1, w_in, *slabs)
    return outs[0], *[o.reshape(w.shape) for o, w in zip(outs[1:], (w_out, w_gu, w_down))]


def _mixer_ffn_kernel(sinks_ref, relb_ref, bucket_ref, p_ref, x_ref, g1_ref,
                      sh2_ref, sc2_ref, g2_ref, ga_ref, cw_ref, gc_ref, wo_ref, gn2_ref, wgu_ref, wd_ref,
                      gf_ref, o_ref, biasm, kcarry, ubuf, attn_sc, merged, x1_sc, x1_res, hbuf, act,
                      *, n_tiles, tiles_per_seq, final_norm):
    tq = p_ref.shape[1]
    nblk = tq // BLOCK
    t = pl.program_id(0)
    i = jnp.minimum(t, n_tiles - 1) % tiles_per_seq
    n_pairs = N_Q_HEADS // 2

    @pl.when(t == 0)
    def _():
        ubuf[...] = jnp.zeros(ubuf.shape, F32)
        kcarry[...] = jnp.zeros(kcarry.shape, BF16)
        bk = bucket_ref[...]
        col = lax.broadcasted_iota(jnp.int32, (BLOCK, 2 * BLOCK), 1)
        for h in range(N_Q_HEADS):
            acc = jnp.full((BLOCK, 2 * BLOCK), NEG_INF, F32)
            for bb in range(N_BUCKETS):
                acc = jnp.where(bk == bb, relb_ref[bb, h] * LOG2E, acc)
            first = jnp.where(col < BLOCK, NEG_INF, acc)
            sink = sinks_ref[h] * LOG2E
            c0 = (h % 2) * 2 * BLOCK
            biasm[h // 2, :, c0:c0 + 2 * BLOCK] = jnp.where(col == 0, sink, acc)
            biasm[n_pairs + h // 2, :, c0:c0 + 2 * BLOCK] = jnp.where(col == 0, sink, first)

    gain = gn2_ref[...] * (1.0 + sc2_ref[0])
    shift = sh2_ref[0]
    subs = [slice(r0, r0 + FFN_SUB) for r0 in range(0, tq, FFN_SUB)]

    def ffn_norm(rows):
        x1 = x1_sc[rows, :]
        x1_res[rows, :] = x1
        hbuf[rows, :] = ((x1 * _rms_scale(x1, D_MODEL)) * gain + shift).astype(BF16)

    def ffn_chunk(rows, c0):
        c1 = min(c0 + FF_CHUNK, D_FF)
        h = hbuf[rows, :]
        gate = jnp.dot(h, wgu_ref[:, c0:c1], preferred_element_type=F32)
        up = jnp.dot(h, wgu_ref[:, D_FF + c0:D_FF + c1], preferred_element_type=F32)
        act[rows, c0:c1] = (gate * jax.nn.sigmoid(gate) * up).astype(BF16)

    def ffn_down(rows):
        y = jnp.dot(act[rows, :], wd_ref[...], preferred_element_type=F32)
        x2 = x1_res[rows, :] + g2_ref[0] * y
        o_ref[0, rows, :] = x2 * _rms_scale(x2, D_MODEL) * gf_ref[...] if final_norm else x2

    ffn_units = []
    for rows in subs:
        ffn_units += [functools.partial(ffn_chunk, rows, c0) for c0 in range(0, D_FF, FF_CHUNK)]
        ffn_units.append(functools.partial(ffn_down, rows))

    lane = lax.broadcasted_iota(jnp.int32, (2 * BLOCK, LANES), 1)
    real_key = lax.broadcasted_iota(jnp.int32, (2 * BLOCK, LANES), 0) > 0
    keep_lo = (lane < HEAD_DIM) & real_key
    keep_hi = (lane >= HEAD_DIM) & real_key
    zero = jnp.zeros((2 * BLOCK, LANES), BF16)
    ones_lo = jnp.where(lane < HEAD_DIM, 1.0, 0.0).astype(BF16)
    ones_hi = jnp.where(lane >= HEAD_DIM, 1.0, 0.0).astype(BF16)

    def attn_group(j, kh):
        r0 = j * BLOCK
        table = jnp.where(i == 0, n_pairs, 0) if j == 0 else 0
        if j == 0:
            kvw = jnp.concatenate([kcarry[...], p_ref[0, 0:BLOCK, KV_COL:GATE_COL]], axis=0)
        else:
            kvw = p_ref[0, r0 - BLOCK:r0 + BLOCK, KV_COL:GATE_COL]
        k = kvw[:, :LANES]
        v = kvw[:, LANES:]
        kr = pltpu.roll(k, HEAD_DIM, 1)
        vr = pltpu.roll(v, HEAD_DIM, 1)
        k_lo, k_hi = (k, kr) if kh == 0 else (kr, k)
        v_lo, v_hi = (v, vr) if kh == 0 else (vr, v)
        kx = jnp.concatenate([jnp.where(keep_lo, k_lo, zero), jnp.where(keep_hi, k_hi, zero)], axis=0)
        vx = jnp.concatenate(
            [jnp.concatenate([jnp.where(keep_lo, v_lo, zero), ones_lo], axis=1),
             jnp.concatenate([jnp.where(keep_hi, v_hi, zero), ones_hi], axis=1)], axis=0)
        for pp in range(2):
            pair = kh * 2 + pp
            qp = p_ref[0, r0:r0 + BLOCK, pair * LANES:(pair + 1) * LANES]
            sc = lax.dot_general(qp, kx, (((1,), (1,)), ((), ())), preferred_element_type=F32)
            sc = sc + biasm[table + pair]
            es = []
            for half in range(2):
                st = sc[:, half * 2 * BLOCK:(half + 1) * 2 * BLOCK]
                m = jnp.max(st, axis=-1, keepdims=True)
                es.append(jnp.exp2(st - m).astype(BF16))
            pv = jnp.dot(jnp.concatenate(es, axis=1), vx, preferred_element_type=F32)
            attn_sc[r0:r0 + BLOCK, pair * LANES:(pair + 1) * LANES] = pv[:, :LANES] * (1.0 / pv[:, LANES:])

    def conv_group(rows):
        n = rows.stop - rows.start
        gate_b = p_ref[0, rows, GATE_COL:U_COL].astype(F32)
        u = p_ref[0, rows, U_COL:].astype(F32)
        r0 = rows.start + SUBLANES
        ubuf[r0:r0 + n, :] = u
        ue = ubuf[r0 - SUBLANES:r0 + n, :]
        u1 = pltpu.roll(ue, 1, 0)[SUBLANES:, :]
        u2 = pltpu.roll(ue, 2, 0)[SUBLANES:, :]
        conv = gate_b * (cw_ref[0:1, :] * u2 + cw_ref[1:2, :] * u1 + cw_ref[2:3, :] * u)
        merged[rows, ATTN_WIDTH:] = (conv * _rms_scale(conv, CONV_WIDTH) * gc_ref[...]).astype(BF16)

    def attn_finish(j):
        rows = slice(j * BLOCK, (j + 1) * BLOCK)
        a = attn_sc[rows, :]
        merged[rows, 0:ATTN_WIDTH] = (a * _rms_scale(a, ATTN_WIDTH) * ga_ref[...]).astype(BF16)

    def outproj(rows):
        y = jnp.dot(merged[rows, :], wo_ref[...], preferred_element_type=F32)
        x1_sc[rows, :] = x_ref[0, rows, :] + g1_ref[0] * y

    blocks_per_chunk = OUTPROJ_ROWS // BLOCK
    attn_units = []
    half = BLOCK // 2
    for j in range(nblk + 1):
        head = []
        if j > 0 and j % blocks_per_chunk == 0:
            head.append(functools.partial(outproj, slice((j - blocks_per_chunk) * BLOCK, j * BLOCK)))
        if j < nblk:
            head += [functools.partial(conv_group, slice(j * BLOCK, j * BLOCK + half)),
                     functools.partial(attn_group, j, 0)]
        attn_units.append(lambda fs=head: [f() for f in fs])
        if j < nblk:
            tail = [functools.partial(conv_group, slice(j * BLOCK + half, (j + 1) * BLOCK)),
                    functools.partial(attn_group, j, 1), functools.partial(attn_finish, j)]
            attn_units.append(lambda fs=tail: [f() for f in fs])

    ubuf[0:SUBLANES, :] = jnp.where(i == 0, 0.0, ubuf[tq:tq + SUBLANES, :])

    @pl.when(t == 0)
    def _():
        for a in attn_units:
            a()

    @pl.when(t > 0)
    def _():
        attn_units[0]()
        for rows in subs:
            ffn_norm(rows)
        rest = attn_units[1:]
        done = 0
        for n, unit in enumerate(ffn_units):
            unit()
            want = min(len(rest), -(-(n + 1) * len(rest) // (len(ffn_units) - 1)))
            for a in rest[done:want]:
                a()
            done = want

    kcarry[...] = p_ref[0, tq - BLOCK:tq, KV_COL:GATE_COL]


def _mixer_ffn(packed, x, mod3, sinks, rel_bias, g_attn, conv_w, g_conv, w_out, g_norm2, w_gu, w_down,
               g_final, final_norm):
    b, s, _ = x.shape
    tq = TQ_ATTN
    nblk = tq // BLOCK
    tps = s // tq
    n_tiles = b * tps
    att = lambda t: (jnp.minimum(t, n_tiles - 1) // tps, jnp.minimum(t, n_tiles - 1) % tps)
    ffn = lambda t: (jnp.maximum(t - 1, 0) // tps, jnp.maximum(t - 1, 0) % tps)
    att_tok = lambda w: pl.BlockSpec((1, tq, w), lambda t: (*att(t), 0))
    att_mod = lambda k: pl.BlockSpec((1, 1, D_MODEL), lambda t: (att(t)[0], 0, k))
    ffn_mod = lambda k: pl.BlockSpec((1, 1, D_MODEL), lambda t: (ffn(t)[0], 0, k))
    const = lambda shape: pl.BlockSpec(shape, lambda t: (0,) * len(shape))
    resident = lambda shape: pl.BlockSpec(shape, lambda t: (0,) * len(shape), pipeline_mode=pl.Buffered(1))
    smem = pl.BlockSpec(memory_space=pltpu.SMEM)
    bucket = jnp.asarray(_bucket_table())
    return pl.pallas_call(
        functools.partial(_mixer_ffn_kernel, n_tiles=n_tiles, tiles_per_seq=tps, final_norm=final_norm),
        grid=(n_tiles + 1,),
        in_specs=[smem, smem, const((BLOCK, 2 * BLOCK)),
                  att_tok(PACK_WIDTH), att_tok(D_MODEL), att_mod(2),
                  ffn_mod(3), ffn_mod(4), ffn_mod(5),
                  const((1, ATTN_WIDTH)), const((CONV_K, CONV_WIDTH)), const((1, CONV_WIDTH)),
                  resident((D_MODEL, D_MODEL)), const((1, D_MODEL)),
                  resident((D_MODEL, 2 * D_FF)), resident((D_FF, D_MODEL)), const((1, D_MODEL))],
        out_specs=pl.BlockSpec((1, tq, D_MODEL), lambda t: (*ffn(t), 0)),
        out_shape=jax.ShapeDtypeStruct((b, s, D_MODEL), F32),
        scratch_shapes=[pltpu.VMEM((N_Q_HEADS, BLOCK, 4 * BLOCK), F32),
                        pltpu.VMEM((BLOCK, 2 * KV_WIDTH), BF16),
                        pltpu.VMEM((tq + SUBLANES, CONV_WIDTH), F32),
                        pltpu.VMEM((tq, ATTN_WIDTH), F32),
                        pltpu.VMEM((tq, D_MODEL), BF16),
                        pltpu.VMEM((tq, D_MODEL), F32),
                        pltpu.VMEM((tq, D_MODEL), F32),
                        pltpu.VMEM((tq, D_MODEL), BF16),
                        pltpu.VMEM((tq, D_FF), BF16)],
        compiler_params=pltpu.CompilerParams(
            dimension_semantics=("arbitrary",),
            vmem_limit_bytes=56 * 1024 * 1024),
        name="mixer_ffn",
    )(sinks, rel_bias, bucket, packed, x, *[mod3] * 4, g_attn, conv_w, g_conv, w_out,
      g_norm2, w_gu, w_down, g_final)


def kernel(x, c, rel_bias, w_ada, b_ada, g_norm1, w_in, sinks, conv_w, g_attn_out, g_conv_out,
           w_out, g_norm2, w_gu, w_down, g_final):
    depth = w_ada.shape[0]
    b = x.shape[0]
    for l in range(depth):
        mod3 = _adaln(c, w_ada[l], b_ada[l]).reshape(b, 1, N_MOD * D_MODEL)
        packed, w_out_bf, w_gu_bf, w_down_bf = _inproj(
            x, mod3, g_norm1[l].reshape(1, D_MODEL), w_in[l].astype(BF16), w_out[l], w_gu[l], w_down[l])
        x = _mixer_ffn(packed, x, mod3, sinks[l], rel_bias, g_attn_out[l].reshape(1, ATTN_WIDTH),
                       conv_w[l], g_conv_out[l].reshape(1, CONV_WIDTH), w_out_bf,
                       g_norm2[l].reshape(1, D_MODEL), w_gu_bf, w_down_bf,
                       g_final.reshape(1, D_MODEL), final_norm=(l == depth - 1))
    return x
```

```python
import functools
import math

import jax
import jax.numpy as jnp
import numpy as np
from jax import lax
from jax.experimental import pallas as pl
from jax.experimental.pallas import tpu as pltpu

D_MODEL = 1024
HEAD_DIM = 64
N_Q_HEADS = 8
N_KV_HEADS = 2
ATTN_WIDTH = N_Q_HEADS * HEAD_DIM
KV_WIDTH = N_KV_HEADS * HEAD_DIM
WINDOW = 128
BLOCK = 128
N_BUCKETS = 32
MAX_DISTANCE = 128
CONV_WIDTH = D_MODEL - ATTN_WIDTH
CONV_K = 3
IN_PROJ_WIDTH = ATTN_WIDTH + 2 * KV_WIDTH + 3 * CONV_WIDTH
KV_COL = ATTN_WIDTH
GATE_COL = KV_COL + 2 * KV_WIDTH
U_COL = GATE_COL + CONV_WIDTH
PACK_WIDTH = U_COL + CONV_WIDTH
D_FF = -(-8 * D_MODEL // (3 * 256)) * 256
N_MOD = 6
EPS = 1e-6
NEG_INF = -1e30
LOG2E = math.log2(math.e)

LANES = 128
SUBLANES = 8
MXU_DIM = 256
VMEM_BYTES = 64 * 1024 * 1024

F32 = jnp.float32
BF16 = jnp.bfloat16

TM_PROJ = 2048
PROJ_SUB = 256
TQ_ATTN = 512
OUTPROJ_ROWS = 256
TM_FFN = 512
FFN_SUB = 256
ADA_TN = 1024
FF_CHUNK = 2 * MXU_DIM


def _bucket_table():
    qi = np.arange(BLOCK, dtype=np.int32)[:, None]
    sj = np.arange(2 * BLOCK, dtype=np.int32)[None, :]
    dist = qi + BLOCK - sj
    d0 = np.maximum(dist, 0)
    max_exact = N_BUCKETS // 2
    d = np.maximum(d0, 1).astype(np.float32)
    large = max_exact + (np.log(d / np.float32(max_exact)) / np.float32(math.log(MAX_DISTANCE / max_exact))
                         * np.float32(N_BUCKETS - max_exact)).astype(np.int32)
    large = np.minimum(large, N_BUCKETS - 1)
    bucket = np.where(d0 < max_exact, d0, large)
    in_window = (dist >= 0) & (dist < WINDOW)
    return np.where(in_window, bucket, -1).astype(np.int32)


def _rms_scale(v, width):
    return lax.rsqrt(jnp.sum(v * v, axis=-1, keepdims=True) * (1.0 / width) + EPS)


def _adaln_kernel(c_ref, w_ref, b_ref, o_ref):
    c = c_ref[...]
    cond = c * jax.nn.sigmoid(c)
    o_ref[...] = jnp.dot(cond.astype(BF16), w_ref[...].astype(BF16),
                         preferred_element_type=F32) + b_ref[...]


def _adaln(c, w_ada, b_ada):
    b = c.shape[0]
    n = w_ada.shape[1]
    return pl.pallas_call(
        _adaln_kernel,
        grid=(n // ADA_TN,),
        in_specs=[pl.BlockSpec((b, D_MODEL), lambda j: (0, 0)),
                  pl.BlockSpec((D_MODEL, ADA_TN), lambda j: (0, j)),
                  pl.BlockSpec((1, ADA_TN), lambda j: (0, j))],
        out_specs=pl.BlockSpec((b, ADA_TN), lambda j: (0, j)),
        out_shape=jax.ShapeDtypeStruct((b, n), F32),
        name="adaln_mod",
    )(c, w_ada, b_ada.reshape(1, n))


def _inproj_kernel(x_ref, sh_ref, sc_ref, gn_ref, w_ref, wo_ref, wgu_ref, wd_ref,
                   p_ref, wo_bf_ref, wgu_bf_ref, wd_bf_ref):
    tm = x_ref.shape[1]

    wo_bf_ref[...] = wo_ref[...].astype(BF16)
    wgu_bf_ref[...] = wgu_ref[...].astype(BF16)
    wd_bf_ref[...] = wd_ref[...].astype(BF16)

    gain = gn_ref[...] * (1.0 + sc_ref[0])
    shift = sh_ref[0]
    subs = [slice(r0, r0 + PROJ_SUB) for r0 in range(0, tm, PROJ_SUB)]
    hs = []
    for rows in subs:
        x = x_ref[0, rows, :]
        hs.append(((x * _rms_scale(x, D_MODEL)) * gain + shift).astype(BF16))
    for rows, h in zip(subs, hs):
        proj = jnp.dot(h, w_ref[...], preferred_element_type=F32)
        p_ref[0, rows, :KV_COL] = (proj[:, :KV_COL] * (HEAD_DIM ** -0.5 * LOG2E)).astype(BF16)
        p_ref[0, rows, KV_COL:U_COL] = proj[:, KV_COL:U_COL].astype(BF16)
        p_ref[0, rows, U_COL:] = (proj[:, U_COL:U_COL + CONV_WIDTH] * proj[:, U_COL + CONV_WIDTH:]).astype(BF16)


def _inproj(x, mod3, g_norm1, w_in, w_out, w_gu, w_down):
    b, s, _ = x.shape
    tm = TM_PROJ
    tps = s // tm
    n_steps = b * tps
    tok = lambda w: pl.BlockSpec((1, tm, w), lambda bi, i: (bi, i, 0))
    modspec = lambda k: pl.BlockSpec((1, 1, D_MODEL), lambda bi, i: (bi, 0, k))
    const = lambda shape: pl.BlockSpec(shape, lambda bi, i: (0,) * len(shape))
    slabs = [w.reshape(n_steps, w.shape[0] // n_steps, w.shape[1]) for w in (w_out, w_gu, w_down)]
    slab = lambda w: pl.BlockSpec((1,) + w.shape[1:], lambda bi, i: (bi * tps + i, 0, 0))
    outs = pl.pallas_call(
        _inproj_kernel,
        grid=(b, tps),
        in_specs=[tok(D_MODEL), modspec(0), modspec(1), const((1, D_MODEL)),
                  const((D_MODEL, IN_PROJ_WIDTH)), *[slab(w) for w in slabs]],
        out_specs=[tok(PACK_WIDTH), *[slab(w) for w in slabs]],
        out_shape=[jax.ShapeDtypeStruct((b, s, PACK_WIDTH), BF16),
                   *[jax.ShapeDtypeStruct(w.shape, BF16) for w in slabs]],
        compiler_params=pltpu.CompilerParams(
            dimension_semantics=("parallel", "parallel"),
            vmem_limit_bytes=56 * 1024 * 1024),
        name="inproj",
    )(x, mod3, mod3, g_norm1, w_in, *slabs)
    return outs[0], *[o.reshape(w.shape) for o, w in zip(outs[1:], (w_out, w_gu, w_down))]


def _mixer_ffn_kernel(sinks_ref, relb_ref, bucket_ref, p_ref, x_ref, g1_ref,
                      sh2_ref, sc2_ref, g2_ref, ga_ref, cw_ref, gc_ref, wo_ref, gn2_ref, wgu_ref, wd_ref,
                      gf_ref, o_ref, biasm, kcarry, ubuf, attn_sc, merged, x1_sc, x1_res, hbuf, act,
                      *, n_tiles, tiles_per_seq, final_norm):
    tq = p_ref.shape[1]
    nblk = tq // BLOCK
    t = pl.program_id(0)
    i = jnp.minimum(t, n_tiles - 1) % tiles_per_seq
    n_pairs = N_Q_HEADS // 2

    @pl.when(t == 0)
    def _():
        ubuf[...] = jnp.zeros(ubuf.shape, F32)
        kcarry[...] = jnp.zeros(kcarry.shape, BF16)
        bk = bucket_ref[...]
        col = lax.broadcasted_iota(jnp.int32, (BLOCK, 2 * BLOCK), 1)
        for h in range(N_Q_HEADS):
            acc = jnp.full((BLOCK, 2 * BLOCK), NEG_INF, F32)
            for bb in range(N_BUCKETS):
                acc = jnp.where(bk == bb, relb_ref[bb, h] * LOG2E, acc)
            first = jnp.where(col < BLOCK, NEG_INF, acc)
            sink = sinks_ref[h] * LOG2E
            c0 = (h % 2) * 2 * BLOCK
            biasm[h // 2, :, c0:c0 + 2 * BLOCK] = jnp.where(col == 0, sink, acc)
            biasm[n_pairs + h // 2, :, c0:c0 + 2 * BLOCK] = jnp.where(col == 0, sink, first)

    gain = gn2_ref[...] * (1.0 + sc2_ref[0])
    shift = sh2_ref[0]
    subs = [slice(r0, r0 + FFN_SUB) for r0 in range(0, tq, FFN_SUB)]

    def ffn_norm(rows):
        x1 = x1_sc[rows, :]
        x1_res[rows, :] = x1
        hbuf[rows, :] = ((x1 * _rms_scale(x1, D_MODEL)) * gain + shift).astype(BF16)

    def ffn_chunk(rows, c0):
        c1 = min(c0 + FF_CHUNK, D_FF)
        h = hbuf[rows, :]
        gate = jnp.dot(h, wgu_ref[:, c0:c1], preferred_element_type=F32)
        up = jnp.dot(h, wgu_ref[:, D_FF + c0:D_FF + c1], preferred_element_type=F32)
        act[rows, c0:c1] = (gate * jax.nn.sigmoid(gate) * up).astype(BF16)

    def ffn_down(rows):
        y = jnp.dot(act[rows, :], wd_ref[...], preferred_element_type=F32)
        x2 = x1_res[rows, :] + g2_ref[0] * y
        o_ref[0, rows, :] = x2 * _rms_scale(x2, D_MODEL) * gf_ref[...] if final_norm else x2

    ffn_units = []
    for rows in subs:
        ffn_units += [functools.partial(ffn_chunk, rows, c0) for c0 in range(0, D_FF, FF_CHUNK)]
        ffn_units.append(functools.partial(ffn_down, rows))

    lane = lax.broadcasted_iota(jnp.int32, (2 * BLOCK, LANES), 1)
    real_key = lax.broadcasted_iota(jnp.int32, (2 * BLOCK, LANES), 0) > 0
    keep_lo = (lane < HEAD_DIM) & real_key
    keep_hi = (lane >= HEAD_DIM) & real_key
    zero = jnp.zeros((2 * BLOCK, LANES), BF16)
    ones_lo = jnp.where(lane < HEAD_DIM, 1.0, 0.0).astype(BF16)
    ones_hi = jnp.where(lane >= HEAD_DIM, 1.0, 0.0).astype(BF16)

    def attn_group(j, kh):
        r0 = j * BLOCK
        table = jnp.where(i == 0, n_pairs, 0) if j == 0 else 0
        if j == 0:
            kvw = jnp.concatenate([kcarry[...], p_ref[0, 0:BLOCK, KV_COL:GATE_COL]], axis=0)
        else:
            kvw = p_ref[0, r0 - BLOCK:r0 + BLOCK, KV_COL:GATE_COL]
        k = kvw[:, :LANES]
        v = kvw[:, LANES:]
        kr = pltpu.roll(k, HEAD_DIM, 1)
        vr = pltpu.roll(v, HEAD_DIM, 1)
        k_lo, k_hi = (k, kr) if kh == 0 else (kr, k)
        v_lo, v_hi = (v, vr) if kh == 0 else (vr, v)
        kx = jnp.concatenate([jnp.where(keep_lo, k_lo, zero), jnp.where(keep_hi, k_hi, zero)], axis=0)
        vx = jnp.concatenate(
            [jnp.concatenate([jnp.where(keep_lo, v_lo, zero), ones_lo], axis=1),
             jnp.concatenate([jnp.where(keep_hi, v_hi, zero), ones_hi], axis=1)], axis=0)
        for pp in range(2):
            pair = kh * 2 + pp
            qp = p_ref[0, r0:r0 + BLOCK, pair * LANES:(pair + 1) * LANES]
            sc = lax.dot_general(qp, kx, (((1,), (1,)), ((), ())), preferred_element_type=F32)
            sc = sc + biasm[table + pair]
            es = []
            for half in range(2):
                st = sc[:, half * 2 * BLOCK:(half + 1) * 2 * BLOCK]
                m = jnp.max(st, axis=-1, keepdims=True)
                es.append(jnp.exp2(st - m).astype(BF16))
            pv = jnp.dot(jnp.concatenate(es, axis=1), vx, preferred_element_type=F32)
            attn_sc[r0:r0 + BLOCK, pair * LANES:(pair + 1) * LANES] = pv[:, :LANES] * (1.0 / pv[:, LANES:])

    def conv_group(rows):
        n = rows.stop - rows.start
        gate_b = p_ref[0, rows, GATE_COL:U_COL].astype(F32)
        u = p_ref[0, rows, U_COL:].astype(F32)
        r0 = rows.start + SUBLANES
        ubuf[r0:r0 + n, :] = u
        ue = ubuf[r0 - SUBLANES:r0 + n, :]
        u1 = pltpu.roll(ue, 1, 0)[SUBLANES:, :]
        u2 = pltpu.roll(ue, 2, 0)[SUBLANES:, :]
        conv = gate_b * (cw_ref[0:1, :] * u2 + cw_ref[1:2, :] * u1 + cw_ref[2:3, :] * u)
        merged[rows, ATTN_WIDTH:] = (conv * _rms_scale(conv, CONV_WIDTH) * gc_ref[...]).astype(BF16)

    def attn_finish(j):
        rows = slice(j * BLOCK, (j + 1) * BLOCK)
        a = attn_sc[rows, :]
        merged[rows, 0:ATTN_WIDTH] = (a * _rms_scale(a, ATTN_WIDTH) * ga_ref[...]).astype(BF16)

    def outproj(rows):
        y = jnp.dot(merged[rows, :], wo_ref[...], preferred_element_type=F32)
        x1_sc[rows, :] = x_ref[0, rows, :] + g1_ref[0] * y

    blocks_per_chunk = OUTPROJ_ROWS // BLOCK
    attn_units = []
    half = BLOCK // 2
    for j in range(nblk + 1):
        head = []
        if j > 0 and j % blocks_per_chunk == 0:
            head.append(functools.partial(outproj, slice((j - blocks_per_chunk) * BLOCK, j * BLOCK)))
        if j < nblk:
            head += [functools.partial(conv_group, slice(j * BLOCK, j * BLOCK + half)),
                     functools.partial(attn_group, j, 0)]
        attn_units.append(lambda fs=head: [f() for f in fs])
        if j < nblk:
            tail = [functools.partial(conv_group, slice(j * BLOCK + half, (j + 1) * BLOCK)),
                    functools.partial(attn_group, j, 1), functools.partial(attn_finish, j)]
            attn_units.append(lambda fs=tail: [f() for f in fs])

    ubuf[0:SUBLANES, :] = jnp.where(i == 0, 0.0, ubuf[tq:tq + SUBLANES, :])

    @pl.when(t == 0)
    def _():
        for a in attn_units:
            a()

    @pl.when(t > 0)
    def _():
        attn_units[0]()
        for rows in subs:
            ffn_norm(rows)
        rest = attn_units[1:]
        done = 0
        for n, unit in enumerate(ffn_units):
            unit()
            want = min(len(rest), -(-(n + 1) * len(rest) // (len(ffn_units) - 1)))
            for a in rest[done:want]:
                a()
            done = want

    kcarry[...] = p_ref[0, tq - BLOCK:tq, KV_COL:GATE_COL]


def _mixer_ffn(packed, x, mod3, sinks, rel_bias, g_attn, conv_w, g_conv, w_out, g_norm2, w_gu, w_down,
               g_final, final_norm):
    b, s, _ = x.shape
    tq = TQ_ATTN
    nblk = tq // BLOCK
    tps = s // tq
    n_tiles = b * tps
    att = lambda t: (jnp.minimum(t, n_tiles - 1) // tps, jnp.minimum(t, n_tiles - 1) % tps)
    ffn = lambda t: (jnp.maximum(t - 1, 0) // tps, jnp.maximum(t - 1, 0) % tps)
    att_tok = lambda w: pl.BlockSpec((1, tq, w), lambda t: (*att(t), 0))
    att_mod = lambda k: pl.BlockSpec((1, 1, D_MODEL), lambda t: (att(t)[0], 0, k))
    ffn_mod = lambda k: pl.BlockSpec((1, 1, D_MODEL), lambda t: (ffn(t)[0], 0, k))
    const = lambda shape: pl.BlockSpec(shape, lambda t: (0,) * len(shape))
    resident = lambda shape: pl.BlockSpec(shape, lambda t: (0,) * len(shape), pipeline_mode=pl.Buffered(1))
    smem = pl.BlockSpec(memory_space=pltpu.SMEM)
    bucket = jnp.asarray(_bucket_table())
    return pl.pallas_call(
        functools.partial(_mixer_ffn_kernel, n_tiles=n_tiles, tiles_per_seq=tps, final_norm=final_norm),
        grid=(n_tiles + 1,),
        in_specs=[smem, smem, const((BLOCK, 2 * BLOCK)),
                  att_tok(PACK_WIDTH), att_tok(D_MODEL), att_mod(2),
                  ffn_mod(3), ffn_mod(4), ffn_mod(5),
                  const((1, ATTN_WIDTH)), const((CONV_K, CONV_WIDTH)), const((1, CONV_WIDTH)),
                  resident((D_MODEL, D_MODEL)), const((1, D_MODEL)),
                  resident((D_MODEL, 2 * D_FF)), resident((D_FF, D_MODEL)), const((1, D_MODEL))],
        out_specs=pl.BlockSpec((1, tq, D_MODEL), lambda t: (*ffn(t), 0)),
        out_shape=jax.ShapeDtypeStruct((b, s, D_MODEL), F32),
        scratch_shapes=[pltpu.VMEM((N_Q_HEADS, BLOCK, 4 * BLOCK), F32),
                        pltpu.VMEM((BLOCK, 2 * KV_WIDTH), BF16),
                        pltpu.VMEM((tq + SUBLANES, CONV_WIDTH), F32),
                        pltpu.VMEM((tq, ATTN_WIDTH), F32),
                        pltpu.VMEM((tq, D_MODEL), BF16),
                        pltpu.VMEM((tq, D_MODEL), F32),
                        pltpu.VMEM((tq, D_MODEL), F32),
                        pltpu.VMEM((tq, D_MODEL), BF16),
                        pltpu.VMEM((tq, D_FF), BF16)],
        compiler_params=pltpu.CompilerParams(
            dimension_semantics=("arbitrary",),
            vmem_limit_bytes=56 * 1024 * 1024),
        name="mixer_ffn",
    )(sinks, rel_bias, bucket, packed, x, *[mod3] * 4, g_attn, conv_w, g_conv, w_out,
      g_norm2, w_gu, w_down, g_final)


def kernel(x, c, rel_bias, w_ada, b_ada, g_norm1, w_in, sinks, conv_w, g_attn_out, g_conv_out,
           w_out, g_norm2, w_gu, w_down, g_final):
    depth = w_ada.shape[0]
    b = x.shape[0]
    for l in range(depth):
        mod3 = _adaln(c, w_ada[l], b_ada[l]).reshape(b, 1, N_MOD * D_MODEL)
        packed, w_out_bf, w_gu_bf, w_down_bf = _inproj(
            x, mod3, g_norm1[l].reshape(1, D_MODEL), w_in[l].astype(BF16), w_out[l], w_gu[l], w_down[l])
        x = _mixer_ffn(packed, x, mod3, sinks[l], rel_bias, g_attn_out[l].reshape(1, ATTN_WIDTH),
                       conv_w[l], g_conv_out[l].reshape(1, CONV_WIDTH), w_out_bf,
                       g_norm2[l].reshape(1, D_MODEL), w_gu_bf, w_down_bf,
                       g_final.reshape(1, D_MODEL), final_norm=(l == depth - 1))
    return x
```

```python
import functools
import math

import jax
import jax.numpy as jnp
import numpy as np
from jax import lax
from jax.experimental import pallas as pl
from jax.experimental.pallas import tpu as pltpu

D_MODEL = 1024
HEAD_DIM = 64
N_Q_HEADS = 8
N_KV_HEADS = 2
ATTN_WIDTH = N_Q_HEADS * HEAD_DIM
KV_WIDTH = N_KV_HEADS * HEAD_DIM
WINDOW = 128
BLOCK = 128
N_BUCKETS = 32
MAX_DISTANCE = 128
CONV_WIDTH = D_MODEL - ATTN_WIDTH
CONV_K = 3
IN_PROJ_WIDTH = ATTN_WIDTH + 2 * KV_WIDTH + 3 * CONV_WIDTH
KV_COL = ATTN_WIDTH
GATE_COL = KV_COL + 2 * KV_WIDTH
U_COL = GATE_COL + CONV_WIDTH
PACK_WIDTH = U_COL + CONV_WIDTH
D_FF = -(-8 * D_MODEL // (3 * 256)) * 256
N_MOD = 6
EPS = 1e-6
NEG_INF = -1e30
LOG2E = math.log2(math.e)

LANES = 128
SUBLANES = 8
MXU_DIM = 256
VMEM_BYTES = 64 * 1024 * 1024

F32 = jnp.float32
BF16 = jnp.bfloat16

TM_PROJ = 2048
PROJ_SUB = 256
TQ_ATTN = 512
OUTPROJ_ROWS = 256
TM_FFN = 512
FFN_SUB = 256
ADA_TN = 1024
FF_CHUNK = 2 * MXU_DIM


def _bucket_table():
    qi = np.arange(BLOCK, dtype=np.int32)[:, None]
    sj = np.arange(2 * BLOCK, dtype=np.int32)[None, :]
    dist = qi + BLOCK - sj
    d0 = np.maximum(dist, 0)
    max_exact = N_BUCKETS // 2
    d = np.maximum(d0, 1).astype(np.float32)
    large = max_exact + (np.log(d / np.float32(max_exact)) / np.float32(math.log(MAX_DISTANCE / max_exact))
                         * np.float32(N_BUCKETS - max_exact)).astype(np.int32)
    large = np.minimum(large, N_BUCKETS - 1)
    bucket = np.where(d0 < max_exact, d0, large)
    in_window = (dist >= 0) & (dist < WINDOW)
    return np.where(in_window, bucket, -1).astype(np.int32)


def _rms_scale(v, width):
    return lax.rsqrt(jnp.sum(v * v, axis=-1, keepdims=True) * (1.0 / width) + EPS)


def _adaln_kernel(c_ref, w_ref, b_ref, o_ref):
    c = c_ref[...]
    cond = c * jax.nn.sigmoid(c)
    o_ref[...] = jnp.dot(cond.astype(BF16), w_ref[...].astype(BF16),
                         preferred_element_type=F32) + b_ref[...]


def _adaln(c, w_ada, b_ada):
    b = c.shape[0]
    n = w_ada.shape[1]
    return pl.pallas_call(
        _adaln_kernel,
        grid=(n // ADA_TN,),
        in_specs=[pl.BlockSpec((b, D_MODEL), lambda j: (0, 0)),
                  pl.BlockSpec((D_MODEL, ADA_TN), lambda j: (0, j)),
                  pl.BlockSpec((1, ADA_TN), lambda j: (0, j))],
        out_specs=pl.BlockSpec((b, ADA_TN), lambda j: (0, j)),
        out_shape=jax.ShapeDtypeStruct((b, n), F32),
        name="adaln_mod",
    )(c, w_ada, b_ada.reshape(1, n))


def _inproj_kernel(x_ref, sh_ref, sc_ref, gn_ref, w_ref, wo_ref, wgu_ref, wd_ref,
                   p_ref, wo_bf_ref, wgu_bf_ref, wd_bf_ref):
    tm = x_ref.shape[1]

    wo_bf_ref[...] = wo_ref[...].astype(BF16)
    wgu_bf_ref[...] = wgu_ref[...].astype(BF16)
    wd_bf_ref[...] = wd_ref[...].astype(BF16)

    gain = gn_ref[...] * (1.0 + sc_ref[0])
    shift = sh_ref[0]
    subs = [slice(r0, r0 + PROJ_SUB) for r0 in range(0, tm, PROJ_SUB)]
    hs = []
    for rows in subs:
        x = x_ref[0, rows, :]
        hs.append(((x * _rms_scale(x, D_MODEL)) * gain + shift).astype(BF16))
    for rows, h in zip(subs, hs):
        proj = jnp.dot(h, w_ref[...], preferred_element_type=F32)
        p_ref[0, rows, :KV_COL] = (proj[:, :KV_COL] * (HEAD_DIM ** -0.5 * LOG2E)).astype(BF16)
        p_ref[0, rows, KV_COL:U_COL] = proj[:, KV_COL:U_COL].astype(BF16)
        p_ref[0, rows, U_COL:] = (proj[:, U_COL:U_COL + CONV_WIDTH] * proj[:, U_COL + CONV_WIDTH:]).astype(BF16)


def _inproj(x, mod3, g_norm1, w_in, w_out, w_gu, w_down):
    b, s, _ = x.shape
    tm = TM_PROJ
    tps = s // tm
    n_steps = b * tps
    tok = lambda w: pl.BlockSpec((1, tm, w), lambda bi, i: (bi, i, 0))
    modspec = lambda k: pl.BlockSpec((1, 1, D_MODEL), lambda bi, i: (bi, 0, k))
    const = lambda shape: pl.BlockSpec(shape, lambda bi, i: (0,) * len(shape))
    slabs = [w.reshape(n_steps, w.shape[0] // n_steps, w.shape[1]) for w in (w_out, w_gu, w_down)]
    slab = lambda w: pl.BlockSpec((1,) + w.shape[1:], lambda bi, i: (bi * tps + i, 0, 0))
    outs = pl.pallas_call(
        _inproj_kernel,
        grid=(b, tps),
        in_specs=[tok(D_MODEL), modspec(0), modspec(1), const((1, D_MODEL)),
                  const((D_MODEL, IN_PROJ_WIDTH)), *[slab(w) for w in slabs]],
        out_specs=[tok(PACK_WIDTH), *[slab(w) for w in slabs]],
        out_shape=[jax.ShapeDtypeStruct((b, s, PACK_WIDTH), BF16),
                   *[jax.ShapeDtypeStruct(w.shape, BF16) for w in slabs]],
        compiler_params=pltpu.CompilerParams(
            dimension_semantics=("parallel", "parallel"),
            vmem_limit_bytes=56 * 1024 * 1024),
        name="inproj",
    )(x, mod3, mod3, g_norm1, w_in, *slabs)
    return outs[0], *[o.reshape(w.shape) for o, w in zip(outs[1:], (w_out, w_gu, w_down))]


def _mixer_ffn_kernel(sinks_ref, relb_ref, bucket_ref, p_ref, x_ref, g1_ref,
                      sh2_ref, sc2_ref, g2_ref, ga_ref, cw_ref, gc_ref, wo_ref, gn2_ref, wgu_ref, wd_ref,
                      gf_ref, o_ref, biasm, kcarry, ubuf, attn_sc, merged, x1_sc, x1_res, hbuf, act,
                      *, n_tiles, tiles_per_seq, final_norm):
    tq = p_ref.shape[1]
    nblk = tq // BLOCK
    t = pl.program_id(0)
    i = jnp.minimum(t, n_tiles - 1) % tiles_per_seq
    n_pairs = N_Q_HEADS // 2

    @pl.when(t == 0)
    def _():
        ubuf[...] = jnp.zeros(ubuf.shape, F32)
        kcarry[...] = jnp.zeros(kcarry.shape, BF16)
        bk = bucket_ref[...]
        col = lax.broadcasted_iota(jnp.int32, (BLOCK, 2 * BLOCK), 1)
        for h in range(N_Q_HEADS):
            acc = jnp.full((BLOCK, 2 * BLOCK), NEG_INF, F32)
            for bb in range(N_BUCKETS):
                acc = jnp.where(bk == bb, relb_ref[bb, h] * LOG2E, acc)
            first = jnp.where(col < BLOCK, NEG_INF, acc)
            sink = sinks_ref[h] * LOG2E
            c0 = (h % 2) * 2 * BLOCK
            biasm[h // 2, :, c0:c0 + 2 * BLOCK] = jnp.where(col == 0, sink, acc)
            biasm[n_pairs + h // 2, :, c0:c0 + 2 * BLOCK] = jnp.where(col == 0, sink, first)

    gain = gn2_ref[...] * (1.0 + sc2_ref[0])
    shift = sh2_ref[0]
    subs = [slice(r0, r0 + FFN_SUB) for r0 in range(0, tq, FFN_SUB)]

    def ffn_norm(rows):
        x1 = x1_sc[rows, :]
        x1_res[rows, :] = x1
        hbuf[rows, :] = ((x1 * _rms_scale(x1, D_MODEL)) * gain + shift).astype(BF16)

    def ffn_chunk(rows, c0):
        c1 = min(c0 + FF_CHUNK, D_FF)
        h = hbuf[rows, :]
        gate = jnp.dot(h, wgu_ref[:, c0:c1], preferred_element_type=F32)
        up = jnp.dot(h, wgu_ref[:, D_FF + c0:D_FF + c1], preferred_element_type=F32)
        act[rows, c0:c1] = (gate * jax.nn.sigmoid(gate) * up).astype(BF16)

    def ffn_down(rows):
        y = jnp.dot(act[rows, :], wd_ref[...], preferred_element_type=F32)
        x2 = x1_res[rows, :] + g2_ref[0] * y
        o_ref[0, rows, :] = x2 * _rms_scale(x2, D_MODEL) * gf_ref[...] if final_norm else x2

    ffn_units = []
    for rows in subs:
        ffn_units += [functools.partial(ffn_chunk, rows, c0) for c0 in range(0, D_FF, FF_CHUNK)]
        ffn_units.append(functools.partial(ffn_down, rows))

    lane = lax.broadcasted_iota(jnp.int32, (2 * BLOCK, LANES), 1)
    real_key = lax.broadcasted_iota(jnp.int32, (2 * BLOCK, LANES), 0) > 0
    keep_lo = (lane < HEAD_DIM) & real_key
    keep_hi = (lane >= HEAD_DIM) & real_key
    zero = jnp.zeros((2 * BLOCK, LANES), BF16)
    ones_lo = jnp.where(lane < HEAD_DIM, 1.0, 0.0).astype(BF16)
    ones_hi = jnp.where(lane >= HEAD_DIM, 1.0, 0.0).astype(BF16)

    def attn_group(j, kh):
        r0 = j * BLOCK
        table = jnp.where(i == 0, n_pairs, 0) if j == 0 else 0
        if j == 0:
            kvw = jnp.concatenate([kcarry[...], p_ref[0, 0:BLOCK, KV_COL:GATE_COL]], axis=0)
        else:
            kvw = p_ref[0, r0 - BLOCK:r0 + BLOCK, KV_COL:GATE_COL]
        k = kvw[:, :LANES]
        v = kvw[:, LANES:]
        kr = pltpu.roll(k, HEAD_DIM, 1)
        vr = pltpu.roll(v, HEAD_DIM, 1)
        k_lo, k_hi = (k, kr) if kh == 0 else (kr, k)
        v_lo, v_hi = (v, vr) if kh == 0 else (vr, v)
        kx = jnp.concatenate([jnp.where(keep_lo, k_lo, zero), jnp.where(keep_hi, k_hi, zero)], axis=0)
        vx = jnp.concatenate(
            [jnp.concatenate([jnp.where(keep_lo, v_lo, zero), ones_lo], axis=1),
             jnp.concatenate([jnp.where(keep_hi, v_hi, zero), ones_hi], axis=1)], axis=0)
        for pp in range(2):
            pair = kh * 2 + pp
            qp = p_ref[0, r0:r0 + BLOCK, pair * LANES:(pair + 1) * LANES]
            sc = lax.dot_general(qp, kx, (((1,), (1,)), ((), ())), preferred_element_type=F32)
            sc = sc + biasm[table + pair]
            es = []
            for half in range(2):
                st = sc[:, half * 2 * BLOCK:(half + 1) * 2 * BLOCK]
                m = jnp.max(st, axis=-1, keepdims=True)
                es.append(jnp.exp2(st - m).astype(BF16))
            pv = jnp.dot(jnp.concatenate(es, axis=1), vx, preferred_element_type=F32)
            attn_sc[r0:r0 + BLOCK, pair * LANES:(pair + 1) * LANES] = pv[:, :LANES] * (1.0 / pv[:, LANES:])

    def conv_group(rows):
        n = rows.stop - rows.start
        gate_b = p_ref[0, rows, GATE_COL:U_COL].astype(F32)
        u = p_ref[0, rows, U_COL:].astype(F32)
        r0 = rows.start + SUBLANES
        ubuf[r0:r0 + n, :] = u
        ue = ubuf[r0 - SUBLANES:r0 + n, :]
        u1 = pltpu.roll(ue, 1, 0)[SUBLANES:, :]
        u2 = pltpu.roll(ue, 2, 0)[SUBLANES:, :]
        conv = gate_b * (cw_ref[0:1, :] * u2 + cw_ref[1:2, :] * u1 + cw_ref[2:3, :] * u)
        merged[rows, ATTN_WIDTH:] = (conv * _rms_scale(conv, CONV_WIDTH) * gc_ref[...]).astype(BF16)

    def attn_finish(j):
        rows = slice(j * BLOCK, (j + 1) * BLOCK)
        a = attn_sc[rows, :]
        merged[rows, 0:ATTN_WIDTH] = (a * _rms_scale(a, ATTN_WIDTH) * ga_ref[...]).astype(BF16)

    def outproj(rows):
        y = jnp.dot(merged[rows, :], wo_ref[...], preferred_element_type=F32)
        x1_sc[rows, :] = x_ref[0, rows, :] + g1_ref[0] * y

    blocks_per_chunk = OUTPROJ_ROWS // BLOCK
    attn_units = []
    half = BLOCK // 2
    for j in range(nblk + 1):
        head = []
        if j > 0 and j % blocks_per_chunk == 0:
            head.append(functools.partial(outproj, slice((j - blocks_per_chunk) * BLOCK, j * BLOCK)))
        if j < nblk:
            head += [functools.partial(conv_group, slice(j * BLOCK, j * BLOCK + half)),
                     functools.partial(attn_group, j, 0)]
        attn_units.append(lambda fs=head: [f() for f in fs])
        if j < nblk:
            tail = [functools.partial(conv_group, slice(j * BLOCK + half, (j + 1) * BLOCK)),
                    functools.partial(attn_group, j, 1), functools.partial(attn_finish, j)]
            attn_units.append(lambda fs=tail: [f() for f in fs])

    ubuf[0:SUBLANES, :] = jnp.where(i == 0, 0.0, ubuf[tq:tq + SUBLANES, :])

    @pl.when(t == 0)
    def _():
        for a in attn_units:
            a()

    @pl.when(t > 0)
    def _():
        attn_units[0]()
        for rows in subs:
            ffn_norm(rows)
        rest = attn_units[1:]
        done = 0
        for n, unit in enumerate(ffn_units):
            unit()
            want = (n + 1) * len(rest) // len(ffn_units)
            for a in rest[done:want]:
                a()
            done = want

    kcarry[...] = p_ref[0, tq - BLOCK:tq, KV_COL:GATE_COL]


def _mixer_ffn(packed, x, mod3, sinks, rel_bias, g_attn, conv_w, g_conv, w_out, g_norm2, w_gu, w_down,
               g_final, final_norm):
    b, s, _ = x.shape
    tq = TQ_ATTN
    nblk = tq // BLOCK
    tps = s // tq
    n_tiles = b * tps
    att = lambda t: (jnp.minimum(t, n_tiles - 1) // tps, jnp.minimum(t, n_tiles - 1) % tps)
    ffn = lambda t: (jnp.maximum(t - 1, 0) // tps, jnp.maximum(t - 1, 0) % tps)
    att_tok = lambda w: pl.BlockSpec((1, tq, w), lambda t: (*att(t), 0))
    att_mod = lambda k: pl.BlockSpec((1, 1, D_MODEL), lambda t: (att(t)[0], 0, k))
    ffn_mod = lambda k: pl.BlockSpec((1, 1, D_MODEL), lambda t: (ffn(t)[0], 0, k))
    const = lambda shape: pl.BlockSpec(shape, lambda t: (0,) * len(shape))
    resident = lambda shape: pl.BlockSpec(shape, lambda t: (0,) * len(shape), pipeline_mode=pl.Buffered(1))
    smem = pl.BlockSpec(memory_space=pltpu.SMEM)
    bucket = jnp.asarray(_bucket_table())
    return pl.pallas_call(
        functools.partial(_mixer_ffn_kernel, n_tiles=n_tiles, tiles_per_seq=tps, final_norm=final_norm),
        grid=(n_tiles + 1,),
        in_specs=[smem, smem, const((BLOCK, 2 * BLOCK)),
                  att_tok(PACK_WIDTH), att_tok(D_MODEL), att_mod(2),
                  ffn_mod(3), ffn_mod(4), ffn_mod(5),
                  const((1, ATTN_WIDTH)), const((CONV_K, CONV_WIDTH)), const((1, CONV_WIDTH)),
                  resident((D_MODEL, D_MODEL)), const((1, D_MODEL)),
                  resident((D_MODEL, 2 * D_FF)), resident((D_FF, D_MODEL)), const((1, D_MODEL))],
        out_specs=pl.BlockSpec((1, tq, D_MODEL), lambda t: (*ffn(t), 0)),
        out_shape=jax.ShapeDtypeStruct((b, s, D_MODEL), F32),
        scratch_shapes=[pltpu.VMEM((N_Q_HEADS, BLOCK, 4 * BLOCK), F32),
                        pltpu.VMEM((BLOCK, 2 * KV_WIDTH), BF16),
                        pltpu.VMEM((tq + SUBLANES, CONV_WIDTH), F32),
                        pltpu.VMEM((tq, ATTN_WIDTH), F32),
                        pltpu.VMEM((tq, D_MODEL), BF16),
                        pltpu.VMEM((tq, D_MODEL), F32),
                        pltpu.VMEM((tq, D_MODEL), F32),
                        pltpu.VMEM((tq, D_MODEL), BF16),
                        pltpu.VMEM((tq, D_FF), BF16)],
        compiler_params=pltpu.CompilerParams(
            dimension_semantics=("arbitrary",),
            vmem_limit_bytes=56 * 1024 * 1024),
        name="mixer_ffn",
    )(sinks, rel_bias, bucket, packed, x, *[mod3] * 4, g_attn, conv_w, g_conv, w_out,
      g_norm2, w_gu, w_down, g_final)


def kernel(x, c, rel_bias, w_ada, b_ada, g_norm1, w_in, sinks, conv_w, g_attn_out, g_conv_out,
           w_out, g_norm2, w_gu, w_down, g_final):
    depth = w_ada.shape[0]
    b = x.shape[0]
    for l in range(depth):
        mod3 = _adaln(c, w_ada[l], b_ada[l]).reshape(b, 1, N_MOD * D_MODEL)
        packed, w_out_bf, w_gu_bf, w_down_bf = _inproj(
            x, mod3, g_norm1[l].reshape(1, D_MODEL), w_in[l].astype(BF16), w_out[l], w_gu[l], w_down[l])
        x = _mixer_ffn(packed, x, mod3, sinks[l], rel_bias, g_attn_out[l].reshape(1, ATTN_WIDTH),
                       conv_w[l], g_conv_out[l].reshape(1, CONV_WIDTH), w_out_bf,
                       g_norm2[l].reshape(1, D_MODEL), w_gu_bf, w_down_bf,
                       g_final.reshape(1, D_MODEL), final_norm=(l == depth - 1))
    return x
```

```python
import functools
import math

import jax
import jax.numpy as jnp
import numpy as np
from jax import lax
from jax.experimental import pallas as pl
from jax.experimental.pallas import tpu as pltpu

D_MODEL = 1024
HEAD_DIM = 64
N_Q_HEADS = 8
N_KV_HEADS = 2
ATTN_WIDTH = N_Q_HEADS * HEAD_DIM
KV_WIDTH = N_KV_HEADS * HEAD_DIM
WINDOW = 128
BLOCK = 128
N_BUCKETS = 32
MAX_DISTANCE = 128
CONV_WIDTH = D_MODEL - ATTN_WIDTH
CONV_K = 3
IN_PROJ_WIDTH = ATTN_WIDTH + 2 * KV_WIDTH + 3 * CONV_WIDTH
KV_COL = ATTN_WIDTH
GATE_COL = KV_COL + 2 * KV_WIDTH
U_COL = GATE_COL + CONV_WIDTH
PACK_WIDTH = U_COL + CONV_WIDTH
D_FF = -(-8 * D_MODEL // (3 * 256)) * 256
N_MOD = 6
EPS = 1e-6
NEG_INF = -1e30
LOG2E = math.log2(math.e)

LANES = 128
SUBLANES = 8
MXU_DIM = 256
VMEM_BYTES = 64 * 1024 * 1024

F32 = jnp.float32
BF16 = jnp.bfloat16

TM_PROJ = 2048
PROJ_SUB = 256
TQ_ATTN = 512
OUTPROJ_ROWS = 128
TM_FFN = 512
FFN_SUB = 256
ADA_TN = 1024
FF_CHUNK = 2 * MXU_DIM


def _bucket_table():
    qi = np.arange(BLOCK, dtype=np.int32)[:, None]
    sj = np.arange(2 * BLOCK, dtype=np.int32)[None, :]
    dist = qi + BLOCK - sj
    d0 = np.maximum(dist, 0)
    max_exact = N_BUCKETS // 2
    d = np.maximum(d0, 1).astype(np.float32)
    large = max_exact + (np.log(d / np.float32(max_exact)) / np.float32(math.log(MAX_DISTANCE / max_exact))
                         * np.float32(N_BUCKETS - max_exact)).astype(np.int32)
    large = np.minimum(large, N_BUCKETS - 1)
    bucket = np.where(d0 < max_exact, d0, large)
    in_window = (dist >= 0) & (dist < WINDOW)
    return np.where(in_window, bucket, -1).astype(np.int32)


def _rms_scale(v, width):
    return lax.rsqrt(jnp.sum(v * v, axis=-1, keepdims=True) * (1.0 / width) + EPS)


def _adaln_kernel(c_ref, w_ref, b_ref, o_ref):
    c = c_ref[...]
    cond = c * jax.nn.sigmoid(c)
    o_ref[...] = jnp.dot(cond.astype(BF16), w_ref[...].astype(BF16),
                         preferred_element_type=F32) + b_ref[...]


def _adaln(c, w_ada, b_ada):
    b = c.shape[0]
    n = w_ada.shape[1]
    return pl.pallas_call(
        _adaln_kernel,
        grid=(n // ADA_TN,),
        in_specs=[pl.BlockSpec((b, D_MODEL), lambda j: (0, 0)),
                  pl.BlockSpec((D_MODEL, ADA_TN), lambda j: (0, j)),
                  pl.BlockSpec((1, ADA_TN), lambda j: (0, j))],
        out_specs=pl.BlockSpec((b, ADA_TN), lambda j: (0, j)),
        out_shape=jax.ShapeDtypeStruct((b, n), F32),
        name="adaln_mod",
    )(c, w_ada, b_ada.reshape(1, n))


def _inproj_kernel(x_ref, sh_ref, sc_ref, gn_ref, w_ref, wo_ref, wgu_ref, wd_ref,
                   p_ref, wo_bf_ref, wgu_bf_ref, wd_bf_ref):
    tm = x_ref.shape[1]

    wo_bf_ref[...] = wo_ref[...].astype(BF16)
    wgu_bf_ref[...] = wgu_ref[...].astype(BF16)
    wd_bf_ref[...] = wd_ref[...].astype(BF16)

    gain = gn_ref[...] * (1.0 + sc_ref[0])
    shift = sh_ref[0]
    subs = [slice(r0, r0 + PROJ_SUB) for r0 in range(0, tm, PROJ_SUB)]
    hs = []
    for rows in subs:
        x = x_ref[0, rows, :]
        hs.append(((x * _rms_scale(x, D_MODEL)) * gain + shift).astype(BF16))
    for rows, h in zip(subs, hs):
        proj = jnp.dot(h, w_ref[...], preferred_element_type=F32)
        p_ref[0, rows, :KV_COL] = (proj[:, :KV_COL] * (HEAD_DIM ** -0.5 * LOG2E)).astype(BF16)
        p_ref[0, rows, KV_COL:U_COL] = proj[:, KV_COL:U_COL].astype(BF16)
        p_ref[0, rows, U_COL:] = (proj[:, U_COL:U_COL + CONV_WIDTH] * proj[:, U_COL + CONV_WIDTH:]).astype(BF16)


def _inproj(x, mod3, g_norm1, w_in, w_out, w_gu, w_down):
    b, s, _ = x.shape
    tm = TM_PROJ
    tps = s // tm
    n_steps = b * tps
    tok = lambda w: pl.BlockSpec((1, tm, w), lambda bi, i: (bi, i, 0))
    modspec = lambda k: pl.BlockSpec((1, 1, D_MODEL), lambda bi, i: (bi, 0, k))
    const = lambda shape: pl.BlockSpec(shape, lambda bi, i: (0,) * len(shape))
    slabs = [w.reshape(n_steps, w.shape[0] // n_steps, w.shape[1]) for w in (w_out, w_gu, w_down)]
    slab = lambda w: pl.BlockSpec((1,) + w.shape[1:], lambda bi, i: (bi * tps + i, 0, 0))
    outs = pl.pallas_call(
        _inproj_kernel,
        grid=(b, tps),
        in_specs=[tok(D_MODEL), modspec(0), modspec(1), const((1, D_MODEL)),
                  const((D_MODEL, IN_PROJ_WIDTH)), *[slab(w) for w in slabs]],
        out_specs=[tok(PACK_WIDTH), *[slab(w) for w in slabs]],
        out_shape=[jax.ShapeDtypeStruct((b, s, PACK_WIDTH), BF16),
                   *[jax.ShapeDtypeStruct(w.shape, BF16) for w in slabs]],
        compiler_params=pltpu.CompilerParams(
            dimension_semantics=("parallel", "parallel"),
            vmem_limit_bytes=56 * 1024 * 1024),
        name="inproj",
    )(x, mod3, mod3, g_norm1, w_in, *slabs)
    return outs[0], *[o.reshape(w.shape) for o, w in zip(outs[1:], (w_out, w_gu, w_down))]


def _mixer_ffn_kernel(sinks_ref, relb_ref, bucket_ref, p_ref, x_ref, g1_ref,
                      sh2_ref, sc2_ref, g2_ref, ga_ref, cw_ref, gc_ref, wo_ref, gn2_ref, wgu_ref, wd_ref,
                      gf_ref, o_ref, biasm, kcarry, ubuf, attn_sc, merged, x1_sc, x1_res, hbuf, act,
                      *, n_tiles, tiles_per_seq, final_norm):
    tq = p_ref.shape[1]
    nblk = tq // BLOCK
    t = pl.program_id(0)
    i = jnp.minimum(t, n_tiles - 1) % tiles_per_seq
    n_pairs = N_Q_HEADS // 2

    @pl.when(t == 0)
    def _():
        ubuf[...] = jnp.zeros(ubuf.shape, F32)
        kcarry[...] = jnp.zeros(kcarry.shape, BF16)
        bk = bucket_ref[...]
        col = lax.broadcasted_iota(jnp.int32, (BLOCK, 2 * BLOCK), 1)
        for h in range(N_Q_HEADS):
            acc = jnp.full((BLOCK, 2 * BLOCK), NEG_INF, F32)
            for bb in range(N_BUCKETS):
                acc = jnp.where(bk == bb, relb_ref[bb, h] * LOG2E, acc)
            first = jnp.where(col < BLOCK, NEG_INF, acc)
            sink = sinks_ref[h] * LOG2E
            c0 = (h % 2) * 2 * BLOCK
            biasm[h // 2, :, c0:c0 + 2 * BLOCK] = jnp.where(col == 0, sink, acc)
            biasm[n_pairs + h // 2, :, c0:c0 + 2 * BLOCK] = jnp.where(col == 0, sink, first)

    gain = gn2_ref[...] * (1.0 + sc2_ref[0])
    shift = sh2_ref[0]
    subs = [slice(r0, r0 + FFN_SUB) for r0 in range(0, tq, FFN_SUB)]

    def ffn_norm(rows):
        x1 = x1_sc[rows, :]
        x1_res[rows, :] = x1
        hbuf[rows, :] = ((x1 * _rms_scale(x1, D_MODEL)) * gain + shift).astype(BF16)

    def ffn_chunk(rows, c0):
        c1 = min(c0 + FF_CHUNK, D_FF)
        h = hbuf[rows, :]
        gate = jnp.dot(h, wgu_ref[:, c0:c1], preferred_element_type=F32)
        up = jnp.dot(h, wgu_ref[:, D_FF + c0:D_FF + c1], preferred_element_type=F32)
        act[rows, c0:c1] = (gate * jax.nn.sigmoid(gate) * up).astype(BF16)

    def ffn_down(rows):
        y = jnp.dot(act[rows, :], wd_ref[...], preferred_element_type=F32)
        x2 = x1_res[rows, :] + g2_ref[0] * y
        o_ref[0, rows, :] = x2 * _rms_scale(x2, D_MODEL) * gf_ref[...] if final_norm else x2

    ffn_units = []
    for rows in subs:
        ffn_units += [functools.partial(ffn_chunk, rows, c0) for c0 in range(0, D_FF, FF_CHUNK)]
        ffn_units.append(functools.partial(ffn_down, rows))

    lane = lax.broadcasted_iota(jnp.int32, (2 * BLOCK, LANES), 1)
    real_key = lax.broadcasted_iota(jnp.int32, (2 * BLOCK, LANES), 0) > 0
    keep_lo = (lane < HEAD_DIM) & real_key
    keep_hi = (lane >= HEAD_DIM) & real_key
    zero = jnp.zeros((2 * BLOCK, LANES), BF16)
    ones_lo = jnp.where(lane < HEAD_DIM, 1.0, 0.0).astype(BF16)
    ones_hi = jnp.where(lane >= HEAD_DIM, 1.0, 0.0).astype(BF16)

    def attn_group(j, kh):
        r0 = j * BLOCK
        table = jnp.where(i == 0, n_pairs, 0) if j == 0 else 0
        if j == 0:
            kvw = jnp.concatenate([kcarry[...], p_ref[0, 0:BLOCK, KV_COL:GATE_COL]], axis=0)
        else:
            kvw = p_ref[0, r0 - BLOCK:r0 + BLOCK, KV_COL:GATE_COL]
        k = kvw[:, :LANES]
        v = kvw[:, LANES:]
        kr = pltpu.roll(k, HEAD_DIM, 1)
        vr = pltpu.roll(v, HEAD_DIM, 1)
        k_lo, k_hi = (k, kr) if kh == 0 else (kr, k)
        v_lo, v_hi = (v, vr) if kh == 0 else (vr, v)
        kx = jnp.concatenate([jnp.where(keep_lo, k_lo, zero), jnp.where(keep_hi, k_hi, zero)], axis=0)
        vx = jnp.concatenate(
            [jnp.concatenate([jnp.where(keep_lo, v_lo, zero), ones_lo], axis=1),
             jnp.concatenate([jnp.where(keep_hi, v_hi, zero), ones_hi], axis=1)], axis=0)
        for pp in range(2):
            pair = kh * 2 + pp
            qp = p_ref[0, r0:r0 + BLOCK, pair * LANES:(pair + 1) * LANES]
            sc = lax.dot_general(qp, kx, (((1,), (1,)), ((), ())), preferred_element_type=F32)
            sc = sc + biasm[table + pair]
            es = []
            for half in range(2):
                st = sc[:, half * 2 * BLOCK:(half + 1) * 2 * BLOCK]
                m = jnp.max(st, axis=-1, keepdims=True)
                es.append(jnp.exp2(st - m).astype(BF16))
            pv = jnp.dot(jnp.concatenate(es, axis=1), vx, preferred_element_type=F32)
            attn_sc[r0:r0 + BLOCK, pair * LANES:(pair + 1) * LANES] = pv[:, :LANES] * (1.0 / pv[:, LANES:])

    def conv_group(rows):
        n = rows.stop - rows.start
        gate_b = p_ref[0, rows, GATE_COL:U_COL].astype(F32)
        u = p_ref[0, rows, U_COL:].astype(F32)
        r0 = rows.start + SUBLANES
        ubuf[r0:r0 + n, :] = u
        ue = ubuf[r0 - SUBLANES:r0 + n, :]
        u1 = pltpu.roll(ue, 1, 0)[SUBLANES:, :]
        u2 = pltpu.roll(ue, 2, 0)[SUBLANES:, :]
        conv = gate_b * (cw_ref[0:1, :] * u2 + cw_ref[1:2, :] * u1 + cw_ref[2:3, :] * u)
        merged[rows, ATTN_WIDTH:] = (conv * _rms_scale(conv, CONV_WIDTH) * gc_ref[...]).astype(BF16)

    def attn_finish(j):
        rows = slice(j * BLOCK, (j + 1) * BLOCK)
        a = attn_sc[rows, :]
        merged[rows, 0:ATTN_WIDTH] = (a * _rms_scale(a, ATTN_WIDTH) * ga_ref[...]).astype(BF16)

    def outproj(rows):
        y = jnp.dot(merged[rows, :], wo_ref[...], preferred_element_type=F32)
        x1_sc[rows, :] = x_ref[0, rows, :] + g1_ref[0] * y

    blocks_per_chunk = OUTPROJ_ROWS // BLOCK
    attn_units = []
    half = BLOCK // 2
    for j in range(nblk + 1):
        head = []
        if j > 0 and j % blocks_per_chunk == 0:
            head.append(functools.partial(outproj, slice((j - blocks_per_chunk) * BLOCK, j * BLOCK)))
        if j < nblk:
            head += [functools.partial(conv_group, slice(j * BLOCK, j * BLOCK + half)),
                     functools.partial(attn_group, j, 0)]
        attn_units.append(lambda fs=head: [f() for f in fs])
        if j < nblk:
            tail = [functools.partial(conv_group, slice(j * BLOCK + half, (j + 1) * BLOCK)),
                    functools.partial(attn_group, j, 1), functools.partial(attn_finish, j)]
            attn_units.append(lambda fs=tail: [f() for f in fs])

    ubuf[0:SUBLANES, :] = jnp.where(i == 0, 0.0, ubuf[tq:tq + SUBLANES, :])

    @pl.when(t == 0)
    def _():
        for a in attn_units:
            a()

    @pl.when(t > 0)
    def _():
        attn_units[0]()
        for rows in subs:
            ffn_norm(rows)
        rest = attn_units[1:]
        done = 0
        for n, unit in enumerate(ffn_units):
            unit()
            want = (n + 1) * len(rest) // len(ffn_units)
            for a in rest[done:want]:
                a()
            done = want

    kcarry[...] = p_ref[0, tq - BLOCK:tq, KV_COL:GATE_COL]


def _mixer_ffn(packed, x, mod3, sinks, rel_bias, g_attn, conv_w, g_conv, w_out, g_norm2, w_gu, w_down,
               g_final, final_norm):
    b, s, _ = x.shape
    tq = TQ_ATTN
    nblk = tq // BLOCK
    tps = s // tq
    n_tiles = b * tps
    att = lambda t: (jnp.minimum(t, n_tiles - 1) // tps, jnp.minimum(t, n_tiles - 1) % tps)
    ffn = lambda t: (jnp.maximum(t - 1, 0) // tps, jnp.maximum(t - 1, 0) % tps)
    att_tok = lambda w: pl.BlockSpec((1, tq, w), lambda t: (*att(t), 0))
    att_mod = lambda k: pl.BlockSpec((1, 1, D_MODEL), lambda t: (att(t)[0], 0, k))
    ffn_mod = lambda k: pl.BlockSpec((1, 1, D_MODEL), lambda t: (ffn(t)[0], 0, k))
    const = lambda shape: pl.BlockSpec(shape, lambda t: (0,) * len(shape))
    resident = lambda shape: pl.BlockSpec(shape, lambda t: (0,) * len(shape), pipeline_mode=pl.Buffered(1))
    smem = pl.BlockSpec(memory_space=pltpu.SMEM)
    bucket = jnp.asarray(_bucket_table())
    return pl.pallas_call(
        functools.partial(_mixer_ffn_kernel, n_tiles=n_tiles, tiles_per_seq=tps, final_norm=final_norm),
        grid=(n_tiles + 1,),
        in_specs=[smem, smem, const((BLOCK, 2 * BLOCK)),
                  att_tok(PACK_WIDTH), att_tok(D_MODEL), att_mod(2),
                  ffn_mod(3), ffn_mod(4), ffn_mod(5),
                  const((1, ATTN_WIDTH)), const((CONV_K, CONV_WIDTH)), const((1, CONV_WIDTH)),
                  resident((D_MODEL, D_MODEL)), const((1, D_MODEL)),
                  resident((D_MODEL, 2 * D_FF)), resident((D_FF, D_MODEL)), const((1, D_MODEL))],
        out_specs=pl.BlockSpec((1, tq, D_MODEL), lambda t: (*ffn(t), 0)),
        out_shape=jax.ShapeDtypeStruct((b, s, D_MODEL), F32),
        scratch_shapes=[pltpu.VMEM((N_Q_HEADS, BLOCK, 4 * BLOCK), F32),
                        pltpu.VMEM((BLOCK, 2 * KV_WIDTH), BF16),
                        pltpu.VMEM((tq + SUBLANES, CONV_WIDTH), F32),
                        pltpu.VMEM((tq, ATTN_WIDTH), F32),
                        pltpu.VMEM((tq, D_MODEL), BF16),
                        pltpu.VMEM((tq, D_MODEL), F32),
                        pltpu.VMEM((tq, D_MODEL), F32),
                        pltpu.VMEM((tq, D_MODEL), BF16),
                        pltpu.VMEM((tq, D_FF), BF16)],
        compiler_params=pltpu.CompilerParams(
            dimension_semantics=("arbitrary",),
            vmem_limit_bytes=56 * 1024 * 1024),
        name="mixer_ffn",
    )(sinks, rel_bias, bucket, packed, x, *[mod3] * 4, g_attn, conv_w, g_conv, w_out,
      g_norm2, w_gu, w_down, g_final)


def kernel(x, c, rel_bias, w_ada, b_ada, g_norm1, w_in, sinks, conv_w, g_attn_out, g_conv_out,
           w_out, g_norm2, w_gu, w_down, g_final):
    depth = w_ada.shape[0]
    b = x.shape[0]
    for l in range(depth):
        mod3 = _adaln(c, w_ada[l], b_ada[l]).reshape(b, 1, N_MOD * D_MODEL)
        packed, w_out_bf, w_gu_bf, w_down_bf = _inproj(
            x, mod3, g_norm1[l].reshape(1, D_MODEL), w_in[l].astype(BF16), w_out[l], w_gu[l], w_down[l])
        x = _mixer_ffn(packed, x, mod3, sinks[l], rel_bias, g_attn_out[l].reshape(1, ATTN_WIDTH),
                       conv_w[l], g_conv_out[l].reshape(1, CONV_WIDTH), w_out_bf,
                       g_norm2[l].reshape(1, D_MODEL), w_gu_bf, w_down_bf,
                       g_final.reshape(1, D_MODEL), final_norm=(l == depth - 1))
    return x
```

```python
import functools
import math

import jax
import jax.numpy as jnp
import numpy as np
from jax import lax
from jax.experimental import pallas as pl
from jax.experimental.pallas import tpu as pltpu

D_MODEL = 1024
HEAD_DIM = 64
N_Q_HEADS = 8
N_KV_HEADS = 2
ATTN_WIDTH = N_Q_HEADS * HEAD_DIM
KV_WIDTH = N_KV_HEADS * HEAD_DIM
WINDOW = 128
BLOCK = 128
N_BUCKETS = 32
MAX_DISTANCE = 128
CONV_WIDTH = D_MODEL - ATTN_WIDTH
CONV_K = 3
IN_PROJ_WIDTH = ATTN_WIDTH + 2 * KV_WIDTH + 3 * CONV_WIDTH
KV_COL = ATTN_WIDTH
GATE_COL = KV_COL + 2 * KV_WIDTH
U_COL = GATE_COL + CONV_WIDTH
PACK_WIDTH = U_COL + CONV_WIDTH
D_FF = -(-8 * D_MODEL // (3 * 256)) * 256
N_MOD = 6
N_MOD_EARLY = 2
EPS = 1e-6
NEG_INF = -1e30
LOG2E = math.log2(math.e)

LANES = 128
SUBLANES = 8
MXU_DIM = 256
VMEM_BYTES = 64 * 1024 * 1024
VMEM_LIMIT = VMEM_BYTES * 7 // 8

F32 = jnp.float32
BF16 = jnp.bfloat16

TM_PROJ = 2048
PROJ_SUB = 256
TQ_ATTN = 512
OUTPROJ_ROWS = 128
TM_FFN = 512
FFN_SUB = 256
PREP_STEPS = 8
FF_CHUNK = 2 * MXU_DIM


def _bucket_table():
    qi = np.arange(BLOCK, dtype=np.int32)[:, None]
    sj = np.arange(2 * BLOCK, dtype=np.int32)[None, :]
    dist = qi + BLOCK - sj
    d0 = np.maximum(dist, 0)
    max_exact = N_BUCKETS // 2
    d = np.maximum(d0, 1).astype(np.float32)
    large = max_exact + (np.log(d / np.float32(max_exact)) / np.float32(math.log(MAX_DISTANCE / max_exact))
                         * np.float32(N_BUCKETS - max_exact)).astype(np.int32)
    large = np.minimum(large, N_BUCKETS - 1)
    bucket = np.where(d0 < max_exact, d0, large)
    in_window = (dist >= 0) & (dist < WINDOW)
    return np.where(in_window, bucket, -1).astype(np.int32)


def _rms_scale(v, width):
    return lax.rsqrt(jnp.sum(v * v, axis=-1, keepdims=True) * (1.0 / width) + EPS)


def _modulation(c_ref, w_ref, b_ref):
    c = c_ref[...]
    cond = c * jax.nn.sigmoid(c)
    return jnp.dot(cond.astype(BF16), w_ref[...].astype(BF16), preferred_element_type=F32) + b_ref[...]


def _prep_kernel(c_ref, wada_ref, bada_ref, win_ref, mod_ref, win_bf_ref):
    mod_ref[...] = _modulation(c_ref, wada_ref, bada_ref)
    win_bf_ref[...] = win_ref[...].astype(BF16)


def _prep(c, w_ada, b_ada, w_in):
    b = c.shape[0]
    n_early = N_MOD_EARLY * D_MODEL
    tn = n_early // PREP_STEPS
    rows = w_in.shape[0] // PREP_STEPS
    return pl.pallas_call(
        _prep_kernel,
        grid=(PREP_STEPS,),
        in_specs=[pl.BlockSpec((b, D_MODEL), lambda j: (0, 0)),
                  pl.BlockSpec((D_MODEL, tn), lambda j: (0, j)),
                  pl.BlockSpec((1, tn), lambda j: (0, j)),
                  pl.BlockSpec((rows, w_in.shape[1]), lambda j: (j, 0))],
        out_specs=[pl.BlockSpec((b, tn), lambda j: (0, j)),
                   pl.BlockSpec((rows, w_in.shape[1]), lambda j: (j, 0))],
        out_shape=[jax.ShapeDtypeStruct((b, n_early), F32),
                   jax.ShapeDtypeStruct(w_in.shape, BF16)],
        name="prep",
    )(c, w_ada, b_ada.reshape(1, -1), w_in)


def _inproj_kernel(x_ref, sh_ref, sc_ref, gn_ref, w_ref, c_ref, wada_ref, bada_ref, wo_ref, wgu_ref, wd_ref,
                   p_ref, mod_ref, wo_bf_ref, wgu_bf_ref, wd_bf_ref):
    tm = x_ref.shape[1]

    mod_ref[...] = _modulation(c_ref, wada_ref, bada_ref)
    wo_bf_ref[...] = wo_ref[...].astype(BF16)
    wgu_bf_ref[...] = wgu_ref[...].astype(BF16)
    wd_bf_ref[...] = wd_ref[...].astype(BF16)

    gain = gn_ref[...] * (1.0 + sc_ref[0])
    shift = sh_ref[0]
    subs = [slice(r0, r0 + PROJ_SUB) for r0 in range(0, tm, PROJ_SUB)]
    hs = []
    for rows in subs:
        x = x_ref[0, rows, :]
        hs.append(((x * _rms_scale(x, D_MODEL)) * gain + shift).astype(BF16))
    for rows, h in zip(subs, hs):
        proj = jnp.dot(h, w_ref[...], preferred_element_type=F32)
        p_ref[0, rows, :KV_COL] = (proj[:, :KV_COL] * (HEAD_DIM ** -0.5 * LOG2E)).astype(BF16)
        p_ref[0, rows, KV_COL:U_COL] = proj[:, KV_COL:U_COL].astype(BF16)
        p_ref[0, rows, U_COL:] = (proj[:, U_COL:U_COL + CONV_WIDTH] * proj[:, U_COL + CONV_WIDTH:]).astype(BF16)


def _inproj(x, mod_early, g_norm1, w_in, c, w_ada, b_ada, w_out, w_gu, w_down):
    b, s, _ = x.shape
    tm = TM_PROJ
    tps = s // tm
    n_steps = b * tps
    step = lambda bi, i: bi * tps + i
    tok = lambda w: pl.BlockSpec((1, tm, w), lambda bi, i: (bi, i, 0))
    modspec = lambda k: pl.BlockSpec((1, 1, D_MODEL), lambda bi, i: (bi, 0, k))
    const = lambda shape: pl.BlockSpec(shape, lambda bi, i: (0,) * len(shape))
    n_late = (N_MOD - N_MOD_EARLY) * D_MODEL
    tn = n_late // n_steps
    first = N_MOD_EARLY * D_MODEL // tn
    assert tn % LANES == 0 and tn * n_steps == n_late and first * tn == N_MOD_EARLY * D_MODEL
    assert all(w.shape[0] % n_steps == 0 for w in (w_out, w_gu, w_down))
    ada_slab = lambda rows: pl.BlockSpec((rows, tn), lambda bi, i: (0, first + step(bi, i)))
    slabs = [w.reshape(n_steps, w.shape[0] // n_steps, w.shape[1]) for w in (w_out, w_gu, w_down)]
    slab = lambda w: pl.BlockSpec((1,) + w.shape[1:], lambda bi, i: (step(bi, i), 0, 0))
    outs = pl.pallas_call(
        _inproj_kernel,
        grid=(b, tps),
        in_specs=[tok(D_MODEL), modspec(0), modspec(1), const((1, D_MODEL)),
                  const((D_MODEL, IN_PROJ_WIDTH)),
                  const((b, D_MODEL)), ada_slab(D_MODEL), ada_slab(1), *[slab(w) for w in slabs]],
        out_specs=[tok(PACK_WIDTH), pl.BlockSpec((b, tn), lambda bi, i: (0, step(bi, i))),
                   *[slab(w) for w in slabs]],
        out_shape=[jax.ShapeDtypeStruct((b, s, PACK_WIDTH), BF16),
                   jax.ShapeDtypeStruct((b, n_late), F32),
                   *[jax.ShapeDtypeStruct(w.shape, BF16) for w in slabs]],
        compiler_params=pltpu.CompilerParams(
            dimension_semantics=("parallel", "parallel"),
            vmem_limit_bytes=VMEM_LIMIT),
        name="inproj",
    )(x, mod_early, mod_early, g_norm1, w_in, c, w_ada, b_ada.reshape(1, -1), *slabs)
    return outs[0], outs[1], *[o.reshape(w.shape) for o, w in zip(outs[2:], (w_out, w_gu, w_down))]


def _mixer_ffn_kernel(sinks_ref, relb_ref, bucket_ref, p_ref, x_ref, g1_ref,
                      sh2_ref, sc2_ref, g2_ref, ga_ref, cw_ref, gc_ref, wo_ref, gn2_ref, wgu_ref, wd_ref,
                      gf_ref, o_ref, biasm, kcarry, ubuf, attn_sc, merged, x1_sc, x1_res, hbuf, act,
                      *, n_tiles, tiles_per_seq, final_norm):
    tq = p_ref.shape[1]
    nblk = tq // BLOCK
    t = pl.program_id(0)
    i = jnp.minimum(t, n_tiles - 1) % tiles_per_seq
    n_pairs = N_Q_HEADS // 2

    @pl.when(t == 0)
    def _():
        ubuf[...] = jnp.zeros(ubuf.shape, F32)
        kcarry[...] = jnp.zeros(kcarry.shape, BF16)
        bk = bucket_ref[...]
        col = lax.broadcasted_iota(jnp.int32, (BLOCK, 2 * BLOCK), 1)
        for h in range(N_Q_HEADS):
            acc = jnp.full((BLOCK, 2 * BLOCK), NEG_INF, F32)
            for bb in range(N_BUCKETS):
                acc = jnp.where(bk == bb, relb_ref[bb, h] * LOG2E, acc)
            first = jnp.where(col < BLOCK, NEG_INF, acc)
            sink = sinks_ref[h] * LOG2E
            c0 = (h % 2) * 2 * BLOCK
            biasm[h // 2, :, c0:c0 + 2 * BLOCK] = jnp.where(col == 0, sink, acc)
            biasm[n_pairs + h // 2, :, c0:c0 + 2 * BLOCK] = jnp.where(col == 0, sink, first)

    gain = gn2_ref[...] * (1.0 + sc2_ref[0])
    shift = sh2_ref[0]
    subs = [slice(r0, r0 + FFN_SUB) for r0 in range(0, tq, FFN_SUB)]

    def ffn_norm(rows):
        x1 = x1_sc[rows, :]
        x1_res[rows, :] = x1
        hbuf[rows, :] = ((x1 * _rms_scale(x1, D_MODEL)) * gain + shift).astype(BF16)

    def ffn_chunk(rows, c0):
        c1 = min(c0 + FF_CHUNK, D_FF)
        h = hbuf[rows, :]
        gate = jnp.dot(h, wgu_ref[:, c0:c1], preferred_element_type=F32)
        up = jnp.dot(h, wgu_ref[:, D_FF + c0:D_FF + c1], preferred_element_type=F32)
        act[rows, c0:c1] = (gate * jax.nn.sigmoid(gate) * up).astype(BF16)

    def ffn_down(rows):
        y = jnp.dot(act[rows, :], wd_ref[...], preferred_element_type=F32)
        x2 = x1_res[rows, :] + g2_ref[0] * y
        o_ref[0, rows, :] = x2 * _rms_scale(x2, D_MODEL) * gf_ref[...] if final_norm else x2

    ffn_units = []
    for rows in subs:
        ffn_units += [functools.partial(ffn_chunk, rows, c0) for c0 in range(0, D_FF, FF_CHUNK)]
        ffn_units.append(functools.partial(ffn_down, rows))

    lane = lax.broadcasted_iota(jnp.int32, (2 * BLOCK, LANES), 1)
    real_key = lax.broadcasted_iota(jnp.int32, (2 * BLOCK, LANES), 0) > 0
    keep_lo = (lane < HEAD_DIM) & real_key
    keep_hi = (lane >= HEAD_DIM) & real_key
    zero = jnp.zeros((2 * BLOCK, LANES), BF16)
    ones_lo = jnp.where(lane < HEAD_DIM, 1.0, 0.0).astype(BF16)
    ones_hi = jnp.where(lane >= HEAD_DIM, 1.0, 0.0).astype(BF16)

    def attn_group(j, kh):
        r0 = j * BLOCK
        table = jnp.where(i == 0, n_pairs, 0) if j == 0 else 0
        if j == 0:
            kvw = jnp.concatenate([kcarry[...], p_ref[0, 0:BLOCK, KV_COL:GATE_COL]], axis=0)
        else:
            kvw = p_ref[0, r0 - BLOCK:r0 + BLOCK, KV_COL:GATE_COL]
        k = kvw[:, :LANES]
        v = kvw[:, LANES:]
        kr = pltpu.roll(k, HEAD_DIM, 1)
        vr = pltpu.roll(v, HEAD_DIM, 1)
        k_lo, k_hi = (k, kr) if kh == 0 else (kr, k)
        v_lo, v_hi = (v, vr) if kh == 0 else (vr, v)
        kx = jnp.concatenate([jnp.where(keep_lo, k_lo, zero), jnp.where(keep_hi, k_hi, zero)], axis=0)
        vx = jnp.concatenate(
            [jnp.concatenate([jnp.where(keep_lo, v_lo, zero), ones_lo], axis=1),
             jnp.concatenate([jnp.where(keep_hi, v_hi, zero), ones_hi], axis=1)], axis=0)
        for pp in range(2):
            pair = kh * 2 + pp
            qp = p_ref[0, r0:r0 + BLOCK, pair * LANES:(pair + 1) * LANES]
            sc = lax.dot_general(qp, kx, (((1,), (1,)), ((), ())), preferred_element_type=F32)
            sc = sc + biasm[table + pair]
            es = []
            for half in range(2):
                st = sc[:, half * 2 * BLOCK:(half + 1) * 2 * BLOCK]
                m = jnp.max(st, axis=-1, keepdims=True)
                es.append(jnp.exp2(st - m).astype(BF16))
            pv = jnp.dot(jnp.concatenate(es, axis=1), vx, preferred_element_type=F32)
            attn_sc[r0:r0 + BLOCK, pair * LANES:(pair + 1) * LANES] = pv[:, :LANES] * (1.0 / pv[:, LANES:])

    def conv_group(rows):
        n = rows.stop - rows.start
        gate_b = p_ref[0, rows, GATE_COL:U_COL].astype(F32)
        u = p_ref[0, rows, U_COL:].astype(F32)
        r0 = rows.start + SUBLANES
        ubuf[r0:r0 + n, :] = u
        ue = ubuf[r0 - SUBLANES:r0 + n, :]
        u1 = pltpu.roll(ue, 1, 0)[SUBLANES:, :]
        u2 = pltpu.roll(ue, 2, 0)[SUBLANES:, :]
        conv = gate_b * (cw_ref[0:1, :] * u2 + cw_ref[1:2, :] * u1 + cw_ref[2:3, :] * u)
        merged[rows, ATTN_WIDTH:] = (conv * _rms_scale(conv, CONV_WIDTH) * gc_ref[...]).astype(BF16)

    def attn_finish(j):
        rows = slice(j * BLOCK, (j + 1) * BLOCK)
        a = attn_sc[rows, :]
        merged[rows, 0:ATTN_WIDTH] = (a * _rms_scale(a, ATTN_WIDTH) * ga_ref[...]).astype(BF16)

    def outproj(rows):
        y = jnp.dot(merged[rows, :], wo_ref[...], preferred_element_type=F32)
        x1_sc[rows, :] = x_ref[0, rows, :] + g1_ref[0] * y

    blocks_per_chunk = OUTPROJ_ROWS // BLOCK
    attn_units = []
    half = BLOCK // 2
    for j in range(nblk + 1):
        head = []
        if j > 0 and j % blocks_per_chunk == 0:
            head.append(functools.partial(outproj, slice((j - blocks_per_chunk) * BLOCK, j * BLOCK)))
        if j < nblk:
            head += [functools.partial(conv_group, slice(j * BLOCK, j * BLOCK + half)),
                     functools.partial(attn_group, j, 0)]
        attn_units.append(lambda fs=head: [f() for f in fs])
        if j < nblk:
            tail = [functools.partial(conv_group, slice(j * BLOCK + half, (j + 1) * BLOCK)),
                    functools.partial(attn_group, j, 1), functools.partial(attn_finish, j)]
            attn_units.append(lambda fs=tail: [f() for f in fs])

    ubuf[0:SUBLANES, :] = jnp.where(i == 0, 0.0, ubuf[tq:tq + SUBLANES, :])

    @pl.when(t == 0)
    def _():
        for a in attn_units:
            a()

    @pl.when(t > 0)
    def _():
        attn_units[0]()
        for rows in subs:
            ffn_norm(rows)
        rest = attn_units[1:]
        done = 0
        for n, unit in enumerate(ffn_units):
            unit()
            want = (n + 1) * len(rest) // len(ffn_units)
            for a in rest[done:want]:
                a()
            done = want

    kcarry[...] = p_ref[0, tq - BLOCK:tq, KV_COL:GATE_COL]


def _mixer_ffn(packed, x, mod3, sinks, rel_bias, g_attn, conv_w, g_conv, w_out, g_norm2, w_gu, w_down,
               g_final, final_norm):
    b, s, _ = x.shape
    tq = TQ_ATTN
    tps = s // tq
    n_tiles = b * tps
    att = lambda t: (jnp.minimum(t, n_tiles - 1) // tps, jnp.minimum(t, n_tiles - 1) % tps)
    ffn = lambda t: (jnp.maximum(t - 1, 0) // tps, jnp.maximum(t - 1, 0) % tps)
    att_tok = lambda w: pl.BlockSpec((1, tq, w), lambda t: (*att(t), 0))
    att_mod = lambda k: pl.BlockSpec((1, 1, D_MODEL), lambda t: (att(t)[0], 0, k))
    ffn_mod = lambda k: pl.BlockSpec((1, 1, D_MODEL), lambda t: (ffn(t)[0], 0, k))
    const = lambda shape: pl.BlockSpec(shape, lambda t: (0,) * len(shape))
    resident = lambda shape: pl.BlockSpec(shape, lambda t: (0,) * len(shape), pipeline_mode=pl.Buffered(1))
    smem = pl.BlockSpec(memory_space=pltpu.SMEM)
    bucket = jnp.asarray(_bucket_table())
    return pl.pallas_call(
        functools.partial(_mixer_ffn_kernel, n_tiles=n_tiles, tiles_per_seq=tps, final_norm=final_norm),
        grid=(n_tiles + 1,),
        in_specs=[smem, smem, const((BLOCK, 2 * BLOCK)),
                  att_tok(PACK_WIDTH), att_tok(D_MODEL), att_mod(2 - N_MOD_EARLY),
                  ffn_mod(3 - N_MOD_EARLY), ffn_mod(4 - N_MOD_EARLY), ffn_mod(5 - N_MOD_EARLY),
                  const((1, ATTN_WIDTH)), const((CONV_K, CONV_WIDTH)), const((1, CONV_WIDTH)),
                  resident((D_MODEL, D_MODEL)), const((1, D_MODEL)),
                  resident((D_MODEL, 2 * D_FF)), resident((D_FF, D_MODEL)), const((1, D_MODEL))],
        out_specs=pl.BlockSpec((1, tq, D_MODEL), lambda t: (*ffn(t), 0)),
        out_shape=jax.ShapeDtypeStruct((b, s, D_MODEL), F32),
        scratch_shapes=[pltpu.VMEM((N_Q_HEADS, BLOCK, 4 * BLOCK), F32),
                        pltpu.VMEM((BLOCK, 2 * KV_WIDTH), BF16),
                        pltpu.VMEM((tq + SUBLANES, CONV_WIDTH), F32),
                        pltpu.VMEM((tq, ATTN_WIDTH), F32),
                        pltpu.VMEM((tq, D_MODEL), BF16),
                        pltpu.VMEM((tq, D_MODEL), F32),
                        pltpu.VMEM((tq, D_MODEL), F32),
                        pltpu.VMEM((tq, D_MODEL), BF16),
                        pltpu.VMEM((tq, D_FF), BF16)],
        compiler_params=pltpu.CompilerParams(
            dimension_semantics=("arbitrary",),
            vmem_limit_bytes=VMEM_LIMIT),
        name="mixer_ffn",
    )(sinks, rel_bias, bucket, packed, x, *[mod3] * 4, g_attn, conv_w, g_conv, w_out,
      g_norm2, w_gu, w_down, g_final)


def kernel(x, c, rel_bias, w_ada, b_ada, g_norm1, w_in, sinks, conv_w, g_attn_out, g_conv_out,
           w_out, g_norm2, w_gu, w_down, g_final):
    depth = w_ada.shape[0]
    b, s, d = x.shape
    assert d == D_MODEL and s % TM_PROJ == 0 and s % TQ_ATTN == 0
    assert TM_PROJ % PROJ_SUB == 0 and TQ_ATTN % FFN_SUB == 0 and TQ_ATTN % OUTPROJ_ROWS == 0
    assert OUTPROJ_ROWS % BLOCK == 0 and D_FF % MXU_DIM == 0
    for l in range(depth):
        mod_early, w_in_bf = _prep(c, w_ada[l], b_ada[l], w_in[l])
        packed, mod_late, w_out_bf, w_gu_bf, w_down_bf = _inproj(
            x, mod_early.reshape(b, 1, -1), g_norm1[l].reshape(1, D_MODEL), w_in_bf, c, w_ada[l], b_ada[l],
            w_out[l], w_gu[l], w_down[l])
        x = _mixer_ffn(packed, x, mod_late.reshape(b, 1, -1), sinks[l], rel_bias,
                       g_attn_out[l].reshape(1, ATTN_WIDTH),
                       conv_w[l], g_conv_out[l].reshape(1, CONV_WIDTH), w_out_bf,
                       g_norm2[l].reshape(1, D_MODEL), w_gu_bf, w_down_bf,
                       g_final.reshape(1, D_MODEL), final_norm=(l == depth - 1))
    return x
```
